```python
import math
import jax, jax.numpy as jnp
from jax import lax
import numpy as np

D_MODEL = 2048
BATCH = 8
SEQ = 4096
DEPTH = 2

CHUNK = 64
Q_BLOCK = 128
NORM_EPS = 1e-6
NEG_INF = -1e30

A_HEADS = 8
A_QK_DIM = 64
A_V_DIM = 2 * A_QK_DIM
B_HEADS = 8
B_HEAD_DIM = 128
C_HEADS = 8
C_Q_RANK = 512
C_KV_RANK = 512
C_NOPE_DIM = 128
C_ROPE_DIM = 64
C_V_DIM = 128
ROPE_THETA = 10000.0

N_BRANCHES = 3
BRANCH_WIDTH = 1024

IN_SPLIT_SIZES = (
    A_HEADS * 2 * A_QK_DIM,
    A_HEADS * 2 * A_QK_DIM,
    A_HEADS * A_V_DIM,
    B_HEADS * B_HEAD_DIM,
    B_HEADS * B_HEAD_DIM,
    B_HEADS * B_HEAD_DIM,
    B_HEADS,
    C_Q_RANK,
    C_KV_RANK,
    C_ROPE_DIM,
    N_BRANCHES * D_MODEL,
)
N_IN = 13384

DFF_DENSE = 5632
N_EXPERTS = 8
TOP_K = 2
DFF_EXPERT = 7168

kernel_name = 'hybrid_gated_diff_fox_mla_moe_encoder'


def rmsnorm(x, g):
    xf = x.astype(jnp.float32)
    y = xf * lax.rsqrt(jnp.mean(xf * xf, axis=-1, keepdims=True) + NORM_EPS)
    return (y * g.astype(jnp.float32)).astype(x.dtype)


def rotary(x, positions):
    half = x.shape[-1] // 2
    inv_freq = ROPE_THETA ** (-jnp.arange(half, dtype=jnp.float32) / half)
    ang = positions.astype(jnp.float32)[..., None] * inv_freq
    if x.ndim == 4:
        ang = ang[:, :, None, :]
    cos, sin = jnp.cos(ang), jnp.sin(ang)
    xf = x.astype(jnp.float32)
    x1, x2 = xf[..., :half], xf[..., half:]
    return jnp.concatenate([x1 * cos - x2 * sin, x2 * cos + x1 * sin], axis=-1).astype(x.dtype)


def alibi_slopes(n_heads):
    return jnp.exp2(-8.0 * (jnp.arange(n_heads, dtype=jnp.float32) + 1.0) / n_heads)


def block_causal_mask(q_idx, k_idx):
    return (k_idx[None, :] // CHUNK) <= (q_idx[:, None] // CHUNK)


def sweep_query_blocks(block_fn, seq_len):
    out = lax.map(block_fn, jnp.arange(seq_len // Q_BLOCK))
    nb, b, qb, h, dv = out.shape
    return jnp.moveaxis(out, 0, 1).reshape(b, nb * qb, h, dv)


def diff_attention(q, k, v, positions, lam, lam_init, subln_g):
    b, s_len, h, dv = v.shape
    slopes = alibi_slopes(h)
    k_idx = jnp.arange(s_len)
    scale = A_QK_DIM ** -0.5

    def block(i):
        start = i * Q_BLOCK
        qb = lax.dynamic_slice_in_dim(q, start, Q_BLOCK, axis=1)
        pq = lax.dynamic_slice_in_dim(positions, start, Q_BLOCK, axis=1)
        q_idx = start + jnp.arange(Q_BLOCK)
        s = jnp.einsum('bqhmd,bkhmd->bhmqk', qb, k).astype(jnp.float32) * scale
        dist = jnp.abs(pq[:, :, None] - positions[:, None, :]).astype(jnp.float32)
        s = s - slopes[None, :, None, None, None] * dist[:, None, None]
        s = jnp.where(block_causal_mask(q_idx, k_idx), s, NEG_INF)
        p = jax.nn.softmax(s, axis=-1)
        w = p[:, :, 0] - lam * p[:, :, 1]
        return jnp.einsum('bhqk,bkhd->bqhd', w.astype(v.dtype), v)

    o = sweep_query_blocks(block, s_len)
    o = rmsnorm(o, subln_g) * (1.0 - lam_init)
    return o.reshape(b, s_len, h * dv)


def forgetting_attention(q, k, v, f_logit):
    b, s_len, h, d = v.shape
    cum = jnp.cumsum(jax.nn.log_sigmoid(f_logit.astype(jnp.float32)), axis=1)
    cum_t = jnp.transpose(cum, (0, 2, 1))
    k_idx = jnp.arange(s_len)
    scale = d ** -0.5

    def block(i):
        start = i * Q_BLOCK
        qb = lax.dynamic_slice_in_dim(q, start, Q_BLOCK, axis=1)
        cq = lax.dynamic_slice_in_dim(cum_t, start, Q_BLOCK, axis=2)
        q_idx = start + jnp.arange(Q_BLOCK)
        s = jnp.einsum('bqhd,bkhd->bhqk', qb, k).astype(jnp.float32) * scale
        s = s + (cq[:, :, :, None] - cum_t[:, :, None, :])
        s = jnp.where(k_idx[None, :] <= q_idx[:, None], s, NEG_INF)
        p = jax.nn.softmax(s, axis=-1)
        return jnp.einsum('bhqk,bkhd->bqhd', p.astype(v.dtype), v)

    return sweep_query_blocks(block, s_len).reshape(b, s_len, h * d)


def latent_attention(c_q, c_kv, k_rope, positions, q_norm_g, kv_norm_g, w_uq, w_uk, w_uv):
    b, s_len, _ = c_q.shape
    q = jnp.einsum('bsr,rhd->bshd', rmsnorm(c_q, q_norm_g), w_uq)
    q_nope = q[..., :C_NOPE_DIM]
    q_rope = rotary(q[..., C_NOPE_DIM:], positions)
    ckv = rmsnorm(c_kv, kv_norm_g)
    k_nope = jnp.einsum('bsr,rhd->bshd', ckv, w_uk)
    v = jnp.einsum('bsr,rhd->bshd', ckv, w_uv)
    k_r = rotary(k_rope, positions)
    k_idx = jnp.arange(s_len)
    scale = (C_NOPE_DIM + C_ROPE_DIM) ** -0.5

    def block(i):
        start = i * Q_BLOCK
        qn = lax.dynamic_slice_in_dim(q_nope, start, Q_BLOCK, axis=1)
        qr = lax.dynamic_slice_in_dim(q_rope, start, Q_BLOCK, axis=1)
        q_idx = start + jnp.arange(Q_BLOCK)
        s = (jnp.einsum('bqhd,bkhd->bhqk', qn, k_nope)
             + jnp.einsum('bqhr,bkr->bhqk', qr, k_r)).astype(jnp.float32) * scale
        s = jnp.where(block_causal_mask(q_idx, k_idx), s, NEG_INF)
        p = jax.nn.softmax(s, axis=-1)
        return jnp.einsum('bhqk,bkhd->bqhd', p.astype(v.dtype), v)

    return sweep_query_blocks(block, s_len).reshape(b, s_len, C_HEADS * C_V_DIM)


def hybrid_mixer(h, positions, w_in, diff_lambda, diff_subln_g, fox_forget_bias,
                 mla_q_norm_g, mla_kv_norm_g, mla_w_uq, mla_w_uk, mla_w_uv,
                 w_branch, w_out, lam_init):
    b, s_len, _ = h.shape
    split_points = [int(p) for p in np.cumsum(IN_SPLIT_SIZES)[:-1]]
    (a_q, a_k, a_v, b_q, b_k, b_v, b_f, c_q, c_kv, c_kr, gates) = jnp.split(
        h @ w_in, split_points, axis=-1)

    lam_f = diff_lambda.astype(jnp.float32)
    lam = (jnp.exp(jnp.sum(lam_f[0] * lam_f[1])) - jnp.exp(jnp.sum(lam_f[2] * lam_f[3]))
           + lam_init)
    o_a = diff_attention(a_q.reshape(b, s_len, A_HEADS, 2, A_QK_DIM),
                         a_k.reshape(b, s_len, A_HEADS, 2, A_QK_DIM),
                         a_v.reshape(b, s_len, A_HEADS, A_V_DIM),
                         positions, lam, lam_init, diff_subln_g)
    o_b = forgetting_attention(b_q.reshape(b, s_len, B_HEADS, B_HEAD_DIM),
                               b_k.reshape(b, s_len, B_HEADS, B_HEAD_DIM),
                               b_v.reshape(b, s_len, B_HEADS, B_HEAD_DIM),
                               b_f + fox_forget_bias)
    o_c = latent_attention(c_q, c_kv, c_kr, positions, mla_q_norm_g, mla_kv_norm_g,
                           mla_w_uq, mla_w_uk, mla_w_uv)

    branches = jnp.stack([o_a, o_b, o_c], axis=2)
    branch_out = jnp.einsum('bsgc,gcd->bsgd', branches, w_branch)
    gate = jax.nn.sigmoid(gates.reshape(b, s_len, N_BRANCHES, D_MODEL))
    merged = jnp.sum(gate * branch_out, axis=2)
    return merged @ w_out


def swiglu(h, w1, w3, w2):
    return (jax.nn.silu(h @ w1) * (h @ w3)) @ w2


def moe_swiglu(h, router_w, w1, w3, w2):
    logits = (h @ router_w).astype(jnp.float32)
    top_vals, top_idx = lax.top_k(logits, TOP_K)
    weights = jax.nn.softmax(top_vals, axis=-1)
    combine = jnp.einsum('bsk,bske->bse', weights,
                         jax.nn.one_hot(top_idx, N_EXPERTS, dtype=jnp.float32))
    y = jnp.zeros_like(h)
    for e in range(N_EXPERTS):
        y = y + combine[..., e, None].astype(h.dtype) * swiglu(h, w1[e], w3[e], w2[e])
    return y


def setup_inputs(seed: int = 0) -> dict:
    key = jax.random.key(seed)
    ks = jax.random.split(key, 24)
    n_dense = (DEPTH + 1) // 2
    n_moe = DEPTH // 2

    def normal(k, shape, fan_in):
        return jax.random.normal(k, shape, jnp.float32) * (fan_in ** -0.5)

    def gain(k, shape):
        return 1.0 + 0.02 * jax.random.normal(k, shape, jnp.float32)

    x = jax.random.normal(ks[0], (BATCH, SEQ, D_MODEL), jnp.float32)
    offset = jax.random.randint(ks[1], (BATCH, 1), 0, 16, dtype=jnp.int32) * CHUNK
    positions = offset + jnp.arange(SEQ, dtype=jnp.int32)[None, :]
    return {
        'x': x,
        'positions': positions,
        'attn_norm_g': gain(ks[2], (DEPTH, D_MODEL)),
        'w_in': normal(ks[3], (DEPTH, D_MODEL, N_IN), D_MODEL),
        'diff_lambda': 0.1 * jax.random.normal(ks[4], (DEPTH, 4, A_QK_DIM), jnp.float32),
        'diff_subln_g': gain(ks[5], (DEPTH, A_V_DIM)),
        'fox_forget_bias': jax.random.uniform(ks[6], (DEPTH, B_HEADS), jnp.float32, 1.0, 5.0),
        'mla_q_norm_g': gain(ks[7], (DEPTH, C_Q_RANK)),
        'mla_kv_norm_g': gain(ks[8], (DEPTH, C_KV_RANK)),
        'mla_w_uq': normal(ks[9], (DEPTH, C_Q_RANK, C_HEADS, C_NOPE_DIM + C_ROPE_DIM), C_Q_RANK),
        'mla_w_uk': normal(ks[10], (DEPTH, C_KV_RANK, C_HEADS, C_NOPE_DIM), C_KV_RANK),
        'mla_w_uv': normal(ks[11], (DEPTH, C_KV_RANK, C_HEADS, C_V_DIM), C_KV_RANK),
        'w_branch': normal(ks[12], (DEPTH, N_BRANCHES, BRANCH_WIDTH, D_MODEL), BRANCH_WIDTH),
        'w_out': normal(ks[13], (DEPTH, D_MODEL, D_MODEL), D_MODEL),
        'ffn_norm_g': gain(ks[14], (DEPTH, D_MODEL)),
        'dense_w1': normal(ks[15], (n_dense, D_MODEL, DFF_DENSE), D_MODEL),
        'dense_w3': normal(ks[16], (n_dense, D_MODEL, DFF_DENSE), D_MODEL),
        'dense_w2': normal(ks[17], (n_dense, DFF_DENSE, D_MODEL), DFF_DENSE),
        'router_w': normal(ks[18], (n_moe, D_MODEL, N_EXPERTS), D_MODEL),
        'expert_w1': normal(ks[19], (n_moe, N_EXPERTS, D_MODEL, DFF_EXPERT), D_MODEL),
        'expert_w3': normal(ks[20], (n_moe, N_EXPERTS, D_MODEL, DFF_EXPERT), D_MODEL),
        'expert_w2': normal(ks[21], (n_moe, N_EXPERTS, DFF_EXPERT, D_MODEL), DFF_EXPERT),
        'final_norm_g': gain(ks[22], (D_MODEL,)),
    }


def reference(x, positions, attn_norm_g, w_in, diff_lambda, diff_subln_g, fox_forget_bias,
              mla_q_norm_g, mla_kv_norm_g, mla_w_uq, mla_w_uk, mla_w_uv, w_branch, w_out,
              ffn_norm_g, dense_w1, dense_w3, dense_w2, router_w, expert_w1, expert_w3,
              expert_w2, final_norm_g):
    for l in range(DEPTH):
        lam_init = 0.8 - 0.6 * math.exp(-0.3 * l)
        h = rmsnorm(x, attn_norm_g[l])
        x = x + hybrid_mixer(h, positions, w_in[l], diff_lambda[l], diff_subln_g[l],
                             fox_forget_bias[l], mla_q_norm_g[l], mla_kv_norm_g[l],
                             mla_w_uq[l], mla_w_uk[l], mla_w_uv[l], w_branch[l], w_out[l],
                             lam_init)
        h = rmsnorm(x, ffn_norm_g[l])
        j = l // 2
        if l % 2 == 0:
            x = x + swiglu(h, dense_w1[j], dense_w3[j], dense_w2[j])
        else:
            x = x + moe_swiglu(h, router_w[j], expert_w1[j], expert_w3[j], expert_w2[j])
    return rmsnorm(x, final_norm_g)
```

```python
import functools
import math

import jax
import jax.numpy as jnp
from jax import lax
from jax.experimental import pallas as pl
from jax.experimental.pallas import tpu as pltpu

F32 = jnp.float32
BF16 = jnp.bfloat16

D_MODEL = 2048
CHUNK_SHIFT = 6
NORM_EPS = 1e-6
NEG_INF = -1e30

HEADS = 8
HEAD_DIM = 128
A_QK_DIM = 64
C_RANK = 512
C_NOPE_DIM = 128
C_ROPE_DIM = 64
C_QK_PAD = 256
ROPE_THETA = 10000.0
N_BRANCHES = 3
BRANCH_WIDTH = 1024
N_EXPERTS = 8
LANES = 128

OFF_GATES = 0
OFF_A = 3 * D_MODEL
OFF_B = OFF_A + 3 * BRANCH_WIDTH
OFF_CQ = OFF_B + 3 * BRANCH_WIDTH
OFF_CKV = OFF_CQ + C_RANK
OFF_CKR = OFF_CKV + C_RANK
OFF_BF = OFF_CKR + 2 * LANES
N_PROJ = 13824

VMEM_LIMIT = 56 * 1024 * 1024


def _cparams(sem):
    return pltpu.CompilerParams(dimension_semantics=sem, vmem_limit_bytes=VMEM_LIMIT)


def _rms(x, g):
    return x * lax.rsqrt(jnp.mean(x * x, axis=-1, keepdims=True) + NORM_EPS) * g


def _rmsnorm_kernel(x_ref, g_ref, o_ref):
    o_ref[...] = _rms(x_ref[...].astype(F32), g_ref[...]).astype(o_ref.dtype)


def rmsnorm_call(x, g, out_dtype, tm=512):
    m, d = x.shape
    tm = min(tm, m)
    return pl.pallas_call(
        _rmsnorm_kernel,
        grid=(m // tm,),
        in_specs=[pl.BlockSpec((tm, d), lambda i: (i, 0)), pl.BlockSpec((1, d), lambda i: (0, 0))],
        out_specs=pl.BlockSpec((tm, d), lambda i: (i, 0)),
        out_shape=jax.ShapeDtypeStruct((m, d), out_dtype),
        compiler_params=_cparams(("parallel",)),
        name="rmsnorm",
    )(x, g.reshape(1, d))


def _matmul_kernel(a_ref, b_ref, o_ref):
    o_ref[...] = jnp.dot(a_ref[...], b_ref[...], preferred_element_type=F32).astype(o_ref.dtype)


def matmul_call(a, b, out_dtype, tm=1024, tn=512):
    m, k = a.shape
    _, n = b.shape
    tm = min(tm, m)
    return pl.pallas_call(
        _matmul_kernel,
        grid=(m // tm, n // tn),
        in_specs=[pl.BlockSpec((tm, k), lambda i, j: (i, 0)), pl.BlockSpec((k, tn), lambda i, j: (0, j))],
        out_specs=pl.BlockSpec((tm, tn), lambda i, j: (i, j)),
        out_shape=jax.ShapeDtypeStruct((m, n), out_dtype),
        compiler_params=_cparams(("parallel", "arbitrary")),
        name="in_proj",
    )(a, b)


def _mla_up_kernel(cq_ref, ckv_ref, ckr_ref, pos_ref, gq_ref, gkv_ref, fq_ref, fk_ref,
                   wq_ref, wqp_ref, wk_ref, wv_ref, q_ref, k_ref, v_ref):
    cqn = _rms(cq_ref[...].astype(F32), gq_ref[...]).astype(BF16)
    ckvn = _rms(ckv_ref[...].astype(F32), gkv_ref[...]).astype(BF16)
    pos = pos_ref[...]
    ang_q = pos * fq_ref[...]
    cos_q, sin_q = jnp.cos(ang_q), jnp.sin(ang_q)
    q = jnp.dot(cqn, wq_ref[...], preferred_element_type=F32)
    qp = jnp.dot(cqn, wqp_ref[...], preferred_element_type=F32)
    for h in range(HEADS):
        sl = slice(h * C_QK_PAD, (h + 1) * C_QK_PAD)
        q_ref[:, sl] = (q[:, sl] * cos_q + qp[:, sl] * sin_q).astype(q_ref.dtype)
    ang_k = pos * fk_ref[...]
    ckr = ckr_ref[...].astype(F32)
    kr = (ckr[:, :LANES] * jnp.cos(ang_k) + ckr[:, LANES:] * jnp.sin(ang_k)).astype(k_ref.dtype)
    kn = jnp.dot(ckvn, wk_ref[...], preferred_element_type=F32).astype(k_ref.dtype)
    for h in range(HEADS):
        k_ref[:, h * C_QK_PAD:h * C_QK_PAD + C_NOPE_DIM] = kn[:, h * C_NOPE_DIM:(h + 1) * C_NOPE_DIM]
        k_ref[:, h * C_QK_PAD + C_NOPE_DIM:(h + 1) * C_QK_PAD] = kr
    v_ref[...] = jnp.dot(ckvn, wv_ref[...], preferred_element_type=F32).astype(v_ref.dtype)


def mla_up_call(proj, posf, gq, gkv, fq, fk, wq, wqp, wk, wv, tm=512):
    m = proj.shape[0]
    tm = min(tm, m)
    const = lambda shape: pl.BlockSpec(shape, lambda i: (0, 0))
    return pl.pallas_call(
        _mla_up_kernel,
        grid=(m // tm,),
        in_specs=[
            pl.BlockSpec((tm, C_RANK), lambda i: (i, OFF_CQ // C_RANK)),
            pl.BlockSpec((tm, C_RANK), lambda i: (i, OFF_CKV // C_RANK)),
            pl.BlockSpec((tm, 2 * LANES), lambda i: (i, OFF_CKR // (2 * LANES))),
            pl.BlockSpec((tm, 1), lambda i: (i, 0)),
            const((1, C_RANK)), const((1, C_RANK)), const((1, C_QK_PAD)), const((1, LANES)),
            const(wq.shape), const(wqp.shape), const(wk.shape), const(wv.shape),
        ],
        out_specs=[
            pl.BlockSpec((tm, HEADS * C_QK_PAD), lambda i: (i, 0)),
            pl.BlockSpec((tm, HEADS * C_QK_PAD), lambda i: (i, 0)),
            pl.BlockSpec((tm, HEADS * HEAD_DIM), lambda i: (i, 0)),
        ],
        out_shape=[
            jax.ShapeDtypeStruct((m, HEADS * C_QK_PAD), BF16),
            jax.ShapeDtypeStruct((m, HEADS * C_QK_PAD), BF16),
            jax.ShapeDtypeStruct((m, HEADS * HEAD_DIM), BF16),
        ],
        compiler_params=_cparams(("parallel",)),
        name="mla_up",
    )(proj, proj, proj, posf, gq, gkv, fq, fk, wq, wqp, wk, wv)


def _fox_cumsum_kernel(f_ref, b_ref, o_ref, *, chunk):
    s_len = f_ref.shape[0]
    r = lax.broadcasted_iota(jnp.int32, (chunk, chunk), 0)
    c = lax.broadcasted_iota(jnp.int32, (chunk, chunk), 1)
    tri = (c <= r).astype(F32)
    carry = jnp.zeros((1, LANES), F32)
    for i in range(s_len // chunk):
        x = f_ref[i * chunk:(i + 1) * chunk, :].astype(F32) + b_ref[...]
        ls = jnp.minimum(x, 0.0) - jnp.log(1.0 + jnp.exp(-jnp.abs(x)))
        cum = jnp.dot(tri, ls, preferred_element_type=F32, precision=lax.Precision.HIGHEST) + carry
        o_ref[i * chunk:(i + 1) * chunk, :] = cum
        carry = cum[chunk - 1:chunk, :]


def fox_cumsum_call(proj3, bias_row):
    b, s_len, _ = proj3.shape
    chunk = min(512, s_len)
    return pl.pallas_call(
        functools.partial(_fox_cumsum_kernel, chunk=chunk),
        grid=(b,),
        in_specs=[pl.BlockSpec((None, s_len, LANES), lambda i: (i, 0, OFF_BF // LANES)),
                  pl.BlockSpec((1, LANES), lambda i: (0, 0))],
        out_specs=pl.BlockSpec((None, s_len, LANES), lambda i: (i, 0, 0)),
        out_shape=jax.ShapeDtypeStruct((b, s_len, LANES), F32),
        compiler_params=_cparams(("parallel",)),
        name="fox_cumsum",
    )(proj3, bias_row)


def _flash_loop(q_t, k_ref, v_ref, m_scr, l_scr, acc_scr, *, q0, tq, tk, bias_fn, causal_chunked):
    ncols = q_t.shape[1]
    m_scr[...] = jnp.full(m_scr.shape, NEG_INF, F32)
    l_scr[...] = jnp.zeros(l_scr.shape, F32)
    acc_scr[...] = jnp.zeros(acc_scr.shape, F32)

    def step(j, masked):
        ks = pl.multiple_of(j * tk, tk)
        s = jnp.dot(k_ref[pl.ds(ks, tk), :], q_t, preferred_element_type=F32)
        if bias_fn is not None:
            s = bias_fn(s, ks)
        if masked:
            kidx = ks + lax.broadcasted_iota(jnp.int32, (tk, ncols), 0)
            qidx = q0 + (lax.broadcasted_iota(jnp.int32, (tk, ncols), 1) & (tq - 1))
            if causal_chunked:
                keep = (kidx >> CHUNK_SHIFT) <= (qidx >> CHUNK_SHIFT)
            else:
                keep = kidx <= qidx
            s = jnp.where(keep, s, NEG_INF)
        m_prev = m_scr[...]
        m_new = jnp.maximum(m_prev, jnp.max(s, axis=0, keepdims=True))
        alpha = jnp.exp(m_prev - m_new)
        p = jnp.exp(s - m_new)
        l_scr[...] = alpha * l_scr[...] + jnp.sum(p, axis=0, keepdims=True)
        pv = lax.dot_general(v_ref[pl.ds(ks, tk), :], p.astype(BF16), (((0,), (0,)), ((), ())),
                             preferred_element_type=F32)
        acc_scr[...] = alpha * acc_scr[...] + pv
        m_scr[...] = m_new

    n_full = q0 // tk

    def body(j, carry):
        step(j, False)
        return carry

    lax.fori_loop(0, n_full, body, 0)
    step(n_full, True)


def _lane_tile(x, ncols):
    return x if ncols == x.shape[1] else jnp.concatenate([x] * (ncols // x.shape[1]), axis=1)


def _attn_a_kernel(slopes_ref, q_ref, k_ref, v_ref, posq_ref, posk_ref, lam_ref, g_ref, o_ref,
                   m_scr, l_scr, acc_scr, *, tq, tk, lam_init):
    h = pl.program_id(1)
    q0 = pl.program_id(2) * tq
    q_t = q_ref[...].astype(F32).T
    row = lax.broadcasted_iota(jnp.int32, q_t.shape, 0)
    zero = jnp.zeros_like(q_t)
    q2_t = jnp.concatenate([jnp.where(row < A_QK_DIM, q_t, zero), jnp.where(row >= A_QK_DIM, q_t, zero)],
                           axis=1).astype(BF16)
    slope = slopes_ref[h]
    pq = _lane_tile(posq_ref[...], 2 * tq)

    def bias_fn(s, ks):
        pk = _lane_tile(posk_ref[pl.ds(ks, tk), :], 2 * tq)
        return s - slope * jnp.abs(pq - pk)

    _flash_loop(q2_t, k_ref, v_ref, m_scr, l_scr, acc_scr, q0=q0, tq=tq, tk=tk, bias_fn=bias_fn,
                causal_chunked=True)
    dl = lam_ref[...]
    lam = (jnp.exp(jnp.sum(dl[0:1] * dl[1:2], axis=1, keepdims=True))
           - jnp.exp(jnp.sum(dl[2:3] * dl[3:4], axis=1, keepdims=True)) + lam_init)
    accn = acc_scr[...] * (1.0 / l_scr[...])
    o = (accn[:, :tq] - lam * accn[:, tq:]).T
    o_ref[...] = (_rms(o, g_ref[...]) * (1.0 - lam_init)).astype(o_ref.dtype)


def attn_a_call(proj3, posq_row, posk_rep, slopes, diff_lambda, subln_g, lam_init, tq=128, tk=256):
    b, s_len, _ = proj3.shape
    tk = min(tk, s_len)
    kern = functools.partial(_attn_a_kernel, tq=tq, tk=tk, lam_init=lam_init)
    cb = OFF_A // HEAD_DIM
    return pl.pallas_call(
        kern,
        grid=(b, HEADS, s_len // tq),
        in_specs=[
            pl.BlockSpec(memory_space=pltpu.SMEM),
            pl.BlockSpec((None, tq, HEAD_DIM), lambda bi, h, i: (bi, i, cb + h)),
            pl.BlockSpec((None, s_len, HEAD_DIM), lambda bi, h, i: (bi, 0, cb + HEADS + h)),
            pl.BlockSpec((None, s_len, HEAD_DIM), lambda bi, h, i: (bi, 0, cb + 2 * HEADS + h)),
            pl.BlockSpec((None, 1, tq), lambda bi, h, i: (bi, 0, i)),
            pl.BlockSpec((None, s_len, LANES), lambda bi, h, i: (bi, 0, 0)),
            pl.BlockSpec((4, A_QK_DIM), lambda bi, h, i: (0, 0)),
            pl.BlockSpec((1, HEAD_DIM), lambda bi, h, i: (0, 0)),
        ],
        out_specs=pl.BlockSpec((None, tq, HEAD_DIM), lambda bi, h, i: (bi, i, h)),
        out_shape=jax.ShapeDtypeStruct((b, s_len, HEADS * HEAD_DIM), BF16),
        scratch_shapes=[pltpu.VMEM((1, 2 * tq), F32), pltpu.VMEM((1, 2 * tq), F32),
                        pltpu.VMEM((HEAD_DIM, 2 * tq), F32)],
        compiler_params=_cparams(("parallel", "parallel", "arbitrary")),
        name="attn_diff",
    )(slopes, proj3, proj3, proj3, posq_row, posk_rep, diff_lambda, subln_g.reshape(1, HEAD_DIM))


def _attn_b_kernel(q_ref, k_ref, v_ref, cum_ref, o_ref, col_scr, m_scr, l_scr, acc_scr, *, tq, tk):
    h = pl.program_id(1)
    qi = pl.program_id(2)

    @pl.when(qi == 0)
    def _():
        lane = lax.broadcasted_iota(jnp.int32, cum_ref.shape, 1)
        col = jnp.sum(jnp.where(lane == h, cum_ref[...], 0.0), axis=1, keepdims=True)
        col_scr[...] = jnp.broadcast_to(col, col_scr.shape)

    q_t = q_ref[...].astype(F32).T.astype(BF16)

    def bias_fn(s, ks):
        return s - _lane_tile(col_scr[pl.ds(ks, tk), :], tq)

    _flash_loop(q_t, k_ref, v_ref, m_scr, l_scr, acc_scr, q0=qi * tq, tq=tq, tk=tk, bias_fn=bias_fn,
                causal_chunked=False)
    o_ref[...] = (acc_scr[...] * (1.0 / l_scr[...])).T.astype(o_ref.dtype)


def attn_b_call(proj3, cum, tq=256, tk=256):
    b, s_len, _ = proj3.shape
    tq, tk = min(tq, s_len), min(tk, s_len)
    cb = OFF_B // HEAD_DIM
    return pl.pallas_call(
        functools.partial(_attn_b_kernel, tq=tq, tk=tk),
        grid=(b, HEADS, s_len // tq),
        in_specs=[
            pl.BlockSpec((None, tq, HEAD_DIM), lambda bi, h, i: (bi, i, cb + h)),
            pl.BlockSpec((None, s_len, HEAD_DIM), lambda bi, h, i: (bi, 0, cb + HEADS + h)),
            pl.BlockSpec((None, s_len, HEAD_DIM), lambda bi, h, i: (bi, 0, cb + 2 * HEADS + h)),
            pl.BlockSpec((None, s_len, LANES), lambda bi, h, i: (bi, 0, 0)),
        ],
        out_specs=pl.BlockSpec((None, tq, HEAD_DIM), lambda bi, h, i: (bi, i, h)),
        out_shape=jax.ShapeDtypeStruct((b, s_len, HEADS * HEAD_DIM), BF16),
        scratch_shapes=[pltpu.VMEM((s_len, LANES), F32), pltpu.VMEM((1, tq), F32),
                        pltpu.VMEM((1, tq), F32), pltpu.VMEM((HEAD_DIM, tq), F32)],
        compiler_params=_cparams(("parallel", "parallel", "arbitrary")),
        name="attn_fox",
    )(proj3, proj3, proj3, cum)


def _attn_c_kernel(q_ref, k_ref, v_ref, o_ref, m_scr, l_scr, acc_scr, *, tq, tk):
    q_t = q_ref[...].astype(F32).T.astype(BF16)
    _flash_loop(q_t, k_ref, v_ref, m_scr, l_scr, acc_scr, q0=pl.program_id(2) * tq, tq=tq, tk=tk,
                bias_fn=None, causal_chunked=True)
    o_ref[...] = (acc_scr[...] * (1.0 / l_scr[...])).T.astype(o_ref.dtype)


def attn_c_call(q3, k3, v3, tq=256, tk=256):
    b, s_len, _ = q3.shape
    tq, tk = min(tq, s_len), min(tk, s_len)
    return pl.pallas_call(
        functools.partial(_attn_c_kernel, tq=tq, tk=tk),
        grid=(b, HEADS, s_len // tq),
        in_specs=[
            pl.BlockSpec((None, tq, C_QK_PAD), lambda bi, h, i: (bi, i, h)),
            pl.BlockSpec((None, s_len, C_QK_PAD), lambda bi, h, i: (bi, 0, h)),
            pl.BlockSpec((None, s_len, HEAD_DIM), lambda bi, h, i: (bi, 0, h)),
        ],
        out_specs=pl.BlockSpec((None, tq, HEAD_DIM), lambda bi, h, i: (bi, i, h)),
        out_shape=jax.ShapeDtypeStruct((b, s_len, HEADS * HEAD_DIM), BF16),
        scratch_shapes=[pltpu.VMEM((1, tq), F32), pltpu.VMEM((1, tq), F32),
                        pltpu.VMEM((HEAD_DIM, tq), F32)],
        compiler_params=_cparams(("parallel", "parallel", "arbitrary")),
        name="attn_mla",
    )(q3, k3, v3)


def _merge_kernel(x_ref, oa_ref, ob_ref, oc_ref, ga_ref, gb_ref, gc_ref, wb_ref, wo_ref, g_ref,
                  x_out_ref, h_out_ref):
    merged = None
    for gi, (o_ref, gate_ref) in enumerate(((oa_ref, ga_ref), (ob_ref, gb_ref), (oc_ref, gc_ref))):
        br = jnp.dot(o_ref[...], wb_ref[gi], preferred_element_type=F32)
        term = jax.nn.sigmoid(gate_ref[...].astype(F32)) * br
        merged = term if merged is None else merged + term
    x_new = x_ref[...] + jnp.dot(merged.astype(BF16), wo_ref[...], preferred_element_type=F32)
    x_out_ref[...] = x_new
    h_out_ref[...] = _rms(x_new, g_ref[...]).astype(h_out_ref.dtype)


def merge_call(x, o_a, o_b, o_c, proj, w_branch, w_out, norm_g, tm=256):
    m, d = x.shape
    tm = min(tm, m)
    row = lambda width: pl.BlockSpec((tm, width), lambda i: (i, 0))
    gate = lambda gi: pl.BlockSpec((tm, d), lambda i: (i, OFF_GATES // d + gi))
    return pl.pallas_call(
        _merge_kernel,
        grid=(m // tm,),
        in_specs=[
            row(d), row(BRANCH_WIDTH), row(BRANCH_WIDTH), row(BRANCH_WIDTH), gate(0), gate(1), gate(2),
            pl.BlockSpec(w_branch.shape, lambda i: (0, 0, 0), pipeline_mode=pl.Buffered(1)),
            pl.BlockSpec(w_out.shape, lambda i: (0, 0), pipeline_mode=pl.Buffered(1)),
            pl.BlockSpec((1, d), lambda i: (0, 0)),
        ],
        out_specs=[row(d), row(d)],
        out_shape=[jax.ShapeDtypeStruct((m, d), F32), jax.ShapeDtypeStruct((m, d), BF16)],
        compiler_params=_cparams(("parallel",)),
        name="merge_out",
    )(x, o_a, o_b, o_c, proj, proj, proj, w_branch, w_out, norm_g.reshape(1, d))


def _ffn_kernel(*refs, tiles_per_expert, emit_x):
    if tiles_per_expert:
        h_ref, x_ref, c_ref, w1_ref, w3_ref, w2_ref, g_ref = refs[:7]
        outs = refs[7:]
    else:
        h_ref, x_ref, w1_ref, w3_ref, w2_ref, g_ref = refs[:6]
        c_ref = None
        outs = refs[6:]
    acc_ref = outs[-1]
    f = pl.program_id(1)

    @pl.when(f == 0)
    def _():
        acc_ref[...] = x_ref[...]

    h = h_ref[...]
    a = jnp.dot(h, w1_ref[...], preferred_element_type=F32)
    b = jnp.dot(h, w3_ref[...], preferred_element_type=F32)
    act = a * jax.nn.sigmoid(a) * b
    if c_ref is not None:
        lane = lax.broadcasted_iota(jnp.int32, c_ref.shape, 1)
        act = act * jnp.sum(jnp.where(lane == f // tiles_per_expert, c_ref[...], 0.0), axis=1, keepdims=True)
    acc_ref[...] += jnp.dot(act.astype(BF16), w2_ref[...], preferred_element_type=F32)

    @pl.when(f == pl.num_programs(1) - 1)
    def _():
        x_new = acc_ref[...]
        if emit_x:
            outs[0][...] = x_new
        outs[-2][...] = _rms(x_new, g_ref[...]).astype(outs[-2].dtype)


def ffn_call(h, x, w1, w3, w2, norm_g, *, combine=None, emit_x, norm_dtype, tm=512, tf=512):
    m, d = x.shape
    tm = min(tm, m)
    experts = w1.ndim == 3
    dff = w1.shape[-1]
    tpe = dff // tf
    row = pl.BlockSpec((tm, d), lambda i, f: (i, 0))
    if experts:
        n_f = w1.shape[0] * tpe
        w_in = pl.BlockSpec((None, d, tf), lambda i, f: (f // tpe, 0, f % tpe))
        w_out = pl.BlockSpec((None, tf, d), lambda i, f: (f // tpe, f % tpe, 0))
        in_specs = [row, row, pl.BlockSpec((tm, LANES), lambda i, f: (i, 0)), w_in, w_in, w_out]
        args = (h, x, combine, w1, w3, w2)
    else:
        n_f = tpe
        w_in = pl.BlockSpec((d, tf), lambda i, f: (0, f))
        w_out = pl.BlockSpec((tf, d), lambda i, f: (f, 0))
        in_specs = [row, row, w_in, w_in, w_out]
        args = (h, x, w1, w3, w2)
    in_specs.append(pl.BlockSpec((1, d), lambda i, f: (0, 0)))
    out_specs = [row]
    out_shape = [jax.ShapeDtypeStruct((m, d), norm_dtype)]
    if emit_x:
        out_specs = [row, row]
        out_shape = [jax.ShapeDtypeStruct((m, d), F32)] + out_shape
    return pl.pallas_call(
        functools.partial(_ffn_kernel, tiles_per_expert=tpe if experts else 0, emit_x=emit_x),
        grid=(m // tm, n_f),
        in_specs=in_specs,
        out_specs=out_specs,
        out_shape=out_shape,
        scratch_shapes=[pltpu.VMEM((tm, d), F32)],
        compiler_params=_cparams(("parallel", "arbitrary")),
        name="moe_ffn" if experts else "dense_ffn",
    )(*args, norm_g.reshape(1, d))


def _router_kernel(x_ref, g_ref, w_ref, c_ref):
    h = _rms(x_ref[...], g_ref[...])
    logits = jnp.dot(h, w_ref[...], preferred_element_type=F32, precision=lax.Precision.HIGHEST)
    lane = lax.broadcasted_iota(jnp.int32, logits.shape, 1)
    logits = jnp.where(lane < N_EXPERTS, logits, -jnp.inf)
    m1 = jnp.max(logits, axis=1, keepdims=True)
    i1 = jnp.min(jnp.where(logits == m1, lane, LANES), axis=1, keepdims=True)
    rest = jnp.where(lane == i1, -jnp.inf, logits)
    m2 = jnp.max(rest, axis=1, keepdims=True)
    i2 = jnp.min(jnp.where(rest == m2, lane, LANES), axis=1, keepdims=True)
    e2 = jnp.exp(m2 - m1)
    inv = 1.0 / (1.0 + e2)
    c_ref[...] = jnp.where(lane == i1, inv, 0.0) + jnp.where(lane == i2, e2 * inv, 0.0)


def router_call(x, norm_g, w_pad, tm=512):
    m, d = x.shape
    tm = min(tm, m)
    return pl.pallas_call(
        _router_kernel,
        grid=(m // tm,),
        in_specs=[pl.BlockSpec((tm, d), lambda i: (i, 0)), pl.BlockSpec((1, d), lambda i: (0, 0)),
                  pl.BlockSpec((d, LANES), lambda i: (0, 0))],
        out_specs=pl.BlockSpec((tm, LANES), lambda i: (i, 0)),
        out_shape=jax.ShapeDtypeStruct((m, LANES), F32),
        compiler_params=_cparams(("parallel",)),
        name="router",
    )(x, norm_g.reshape(1, d), w_pad)


def _rot_pair(w):
    half = w.shape[-1] // 2
    return jnp.concatenate([-w[..., half:], w[..., :half]], axis=-1)


def _pack_in_proj(w):
    d = w.shape[0]
    bw = BRANCH_WIDTH
    aq, ak, av, bq, bk, bv = (w[:, i * bw:(i + 1) * bw] for i in range(6))
    o = 6 * bw
    bf = w[:, o:o + HEADS]
    o += HEADS
    cq = w[:, o:o + C_RANK]
    ckv = w[:, o + C_RANK:o + 2 * C_RANK]
    o += 2 * C_RANK
    ckr = w[:, o:o + C_ROPE_DIM]
    gates = w[:, o + C_ROPE_DIM:]
    z = lambda n: jnp.zeros((d, n), w.dtype)
    packed = jnp.concatenate(
        [gates, aq * (A_QK_DIM ** -0.5), ak, av, bq * (HEAD_DIM ** -0.5), bk, bv, cq, ckv,
         ckr, z(LANES - C_ROPE_DIM), _rot_pair(ckr), z(LANES - C_ROPE_DIM),
         bf, z(LANES - HEADS), z(N_PROJ - OFF_BF - LANES)], axis=1)
    return packed.astype(BF16)


def _pack_mla(w_uq, w_uk, w_uv):
    r = w_uq.shape[0]
    scale = (C_NOPE_DIM + C_ROPE_DIM) ** -0.5
    nope, rope = w_uq[..., :C_NOPE_DIM] * scale, w_uq[..., C_NOPE_DIM:] * scale
    zpad = jnp.zeros((r, HEADS, C_QK_PAD - C_NOPE_DIM - C_ROPE_DIM), w_uq.dtype)
    wq = jnp.concatenate([nope, rope, zpad], axis=-1).reshape(r, HEADS * C_QK_PAD)
    wqp = jnp.concatenate([jnp.zeros_like(nope), _rot_pair(rope), zpad], axis=-1).reshape(r, HEADS * C_QK_PAD)
    return (wq.astype(BF16), wqp.astype(BF16), w_uk.reshape(r, -1).astype(BF16),
            w_uv.reshape(r, -1).astype(BF16))


def _rope_freqs():
    half = C_ROPE_DIM // 2
    inv = ROPE_THETA ** (-jnp.arange(half, dtype=F32) / half)
    fk = jnp.concatenate([inv, inv, jnp.zeros((LANES - C_ROPE_DIM,), F32)])
    fq = jnp.concatenate([jnp.zeros((C_NOPE_DIM,), F32), fk])
    return fq.reshape(1, C_QK_PAD), fk.reshape(1, LANES)


def kernel(x, positions, attn_norm_g, w_in, diff_lambda, diff_subln_g, fox_forget_bias, mla_q_norm_g, mla_kv_norm_g, mla_w_uq, mla_w_uk, mla_w_uv, w_branch, w_out, ffn_norm_g, dense_w1, dense_w3, dense_w2, router_w, expert_w1, expert_w3, expert_w2, final_norm_g):
    b, s_len, d = x.shape
    m = b * s_len
    depth = w_in.shape[0]
    posf = positions.astype(F32)
    posq_row = posf.reshape(b, 1, s_len)
    posk_rep = jnp.broadcast_to(posf[:, :, None], (b, s_len, LANES))
    pos_col = posf.reshape(m, 1)
    slopes = jnp.exp2(-8.0 * (jnp.arange(HEADS, dtype=F32) + 1.0) / HEADS)
    fq, fk = _rope_freqs()

    xr = x.reshape(m, d)
    hn = rmsnorm_call(xr, attn_norm_g[0], BF16)
    out = None
    for l in range(depth):
        lam_init = 0.8 - 0.6 * math.exp(-0.3 * l)
        proj = matmul_call(hn, _pack_in_proj(w_in[l]), BF16)
        proj3 = proj.reshape(b, s_len, N_PROJ)

        o_a = attn_a_call(proj3, posq_row, posk_rep, slopes, diff_lambda[l], diff_subln_g[l], lam_init)
        bias_row = jnp.zeros((1, LANES), F32).at[0, :HEADS].set(fox_forget_bias[l])
        o_b = attn_b_call(proj3, fox_cumsum_call(proj3, bias_row))
        wq, wqp, wk, wv = _pack_mla(mla_w_uq[l], mla_w_uk[l], mla_w_uv[l])
        q_c, k_c, v_c = mla_up_call(proj, pos_col, mla_q_norm_g[l].reshape(1, -1),
                                    mla_kv_norm_g[l].reshape(1, -1), fq, fk, wq, wqp, wk, wv)
        o_c = attn_c_call(q_c.reshape(b, s_len, -1), k_c.reshape(b, s_len, -1), v_c.reshape(b, s_len, -1))

        x1, h_ffn = merge_call(xr, o_a.reshape(m, -1), o_b.reshape(m, -1), o_c.reshape(m, -1), proj,
                               w_branch[l].astype(BF16), w_out[l].astype(BF16), ffn_norm_g[l])
        last = l == depth - 1
        next_g = final_norm_g if last else attn_norm_g[l + 1]
        j = l // 2
        if l % 2 == 0:
            res = ffn_call(h_ffn, x1, dense_w1[j].astype(BF16), dense_w3[j].astype(BF16),
                           dense_w2[j].astype(BF16), next_g, emit_x=not last,
                           norm_dtype=F32 if last else BF16)
        else:
            w_pad = jnp.zeros((d, LANES), F32).at[:, :N_EXPERTS].set(router_w[j])
            combine = router_call(x1, ffn_norm_g[l], w_pad)
            res = ffn_call(h_ffn, x1, expert_w1[j].astype(BF16), expert_w3[j].astype(BF16),
                           expert_w2[j].astype(BF16), next_g, combine=combine, emit_x=not last,
                           norm_dtype=F32 if last else BF16)
        if last:
            out = res[0]
        else:
            xr, hn = res
    return out.reshape(b, s_len, d)
```

```python
import functools
import math

import jax
import jax.numpy as jnp
from jax import lax
from jax.experimental import pallas as pl
from jax.experimental.pallas import tpu as pltpu

F32 = jnp.float32
BF16 = jnp.bfloat16

D_MODEL = 2048
CHUNK_SHIFT = 6
NORM_EPS = 1e-6
NEG_INF = -1e30

HEADS = 8
HEAD_DIM = 128
A_QK_DIM = 64
C_RANK = 512
C_NOPE_DIM = 128
C_ROPE_DIM = 64
C_QK_PAD = 256
ROPE_THETA = 10000.0
N_BRANCHES = 3
BRANCH_WIDTH = 1024
N_EXPERTS = 8
TOP_K = 2
LANES = 128

OFF_GATES = 0
OFF_A = 3 * D_MODEL
OFF_B = OFF_A + 3 * BRANCH_WIDTH
OFF_CQ = OFF_B + 3 * BRANCH_WIDTH
OFF_CKV = OFF_CQ + C_RANK
OFF_CKR = OFF_CKV + C_RANK
OFF_BF = OFF_CKR + 2 * LANES
N_PROJ = 13824

VMEM_LIMIT = 56 * 1024 * 1024


def _cparams(sem):
    return pltpu.CompilerParams(dimension_semantics=sem, vmem_limit_bytes=VMEM_LIMIT)


def _rms(x, g):
    return x * lax.rsqrt(jnp.mean(x * x, axis=-1, keepdims=True) + NORM_EPS) * g


def _rmsnorm_kernel(x_ref, g_ref, o_ref):
    o_ref[...] = _rms(x_ref[...].astype(F32), g_ref[...]).astype(o_ref.dtype)


def rmsnorm_call(x, g, out_dtype, tm=512):
    m, d = x.shape
    tm = min(tm, m)
    return pl.pallas_call(
        _rmsnorm_kernel,
        grid=(m // tm,),
        in_specs=[pl.BlockSpec((tm, d), lambda i: (i, 0)), pl.BlockSpec((1, d), lambda i: (0, 0))],
        out_specs=pl.BlockSpec((tm, d), lambda i: (i, 0)),
        out_shape=jax.ShapeDtypeStruct((m, d), out_dtype),
        compiler_params=_cparams(("parallel",)),
        name="rmsnorm",
    )(x, g.reshape(1, d))


def _matmul_kernel(a_ref, b_ref, o_ref):
    o_ref[...] = jnp.dot(a_ref[...], b_ref[...], preferred_element_type=F32).astype(o_ref.dtype)


def matmul_call(a, b, out_dtype, tm=1024, tn=512):
    m, k = a.shape
    _, n = b.shape
    tm = min(tm, m)
    return pl.pallas_call(
        _matmul_kernel,
        grid=(m // tm, n // tn),
        in_specs=[pl.BlockSpec((tm, k), lambda i, j: (i, 0)), pl.BlockSpec((k, tn), lambda i, j: (0, j))],
        out_specs=pl.BlockSpec((tm, tn), lambda i, j: (i, j)),
        out_shape=jax.ShapeDtypeStruct((m, n), out_dtype),
        compiler_params=_cparams(("parallel", "arbitrary")),
        name="in_proj",
    )(a, b)


def _mla_up_kernel(cq_ref, ckv_ref, ckr_ref, pos_ref, gq_ref, gkv_ref, fq_ref, fk_ref,
                   wq_ref, wqp_ref, wk_ref, wv_ref, q_ref, k_ref, v_ref):
    cqn = _rms(cq_ref[...].astype(F32), gq_ref[...]).astype(BF16)
    ckvn = _rms(ckv_ref[...].astype(F32), gkv_ref[...]).astype(BF16)
    pos = pos_ref[...]
    ang_q = pos * fq_ref[...]
    cos_q, sin_q = jnp.cos(ang_q), jnp.sin(ang_q)
    q = jnp.dot(cqn, wq_ref[...], preferred_element_type=F32)
    qp = jnp.dot(cqn, wqp_ref[...], preferred_element_type=F32)
    for h in range(HEADS):
        sl = slice(h * C_QK_PAD, (h + 1) * C_QK_PAD)
        q_ref[:, sl] = (q[:, sl] * cos_q + qp[:, sl] * sin_q).astype(q_ref.dtype)
    ang_k = pos * fk_ref[...]
    ckr = ckr_ref[...].astype(F32)
    kr = (ckr[:, :LANES] * jnp.cos(ang_k) + ckr[:, LANES:] * jnp.sin(ang_k)).astype(k_ref.dtype)
    kn = jnp.dot(ckvn, wk_ref[...], preferred_element_type=F32).astype(k_ref.dtype)
    for h in range(HEADS):
        k_ref[:, h * C_QK_PAD:h * C_QK_PAD + C_NOPE_DIM] = kn[:, h * C_NOPE_DIM:(h + 1) * C_NOPE_DIM]
        k_ref[:, h * C_QK_PAD + C_NOPE_DIM:(h + 1) * C_QK_PAD] = kr
    v_ref[...] = jnp.dot(ckvn, wv_ref[...], preferred_element_type=F32).astype(v_ref.dtype)


def mla_up_call(proj, posf, gq, gkv, fq, fk, wq, wqp, wk, wv, tm=512):
    m = proj.shape[0]
    tm = min(tm, m)
    const = lambda shape: pl.BlockSpec(shape, lambda i: (0, 0))
    return pl.pallas_call(
        _mla_up_kernel,
        grid=(m // tm,),
        in_specs=[
            pl.BlockSpec((tm, C_RANK), lambda i: (i, OFF_CQ // C_RANK)),
            pl.BlockSpec((tm, C_RANK), lambda i: (i, OFF_CKV // C_RANK)),
            pl.BlockSpec((tm, 2 * LANES), lambda i: (i, OFF_CKR // (2 * LANES))),
            pl.BlockSpec((tm, 1), lambda i: (i, 0)),
            const((1, C_RANK)), const((1, C_RANK)), const((1, C_QK_PAD)), const((1, LANES)),
            const(wq.shape), const(wqp.shape), const(wk.shape), const(wv.shape),
        ],
        out_specs=[
            pl.BlockSpec((tm, HEADS * C_QK_PAD), lambda i: (i, 0)),
            pl.BlockSpec((tm, HEADS * C_QK_PAD), lambda i: (i, 0)),
            pl.BlockSpec((tm, HEADS * HEAD_DIM), lambda i: (i, 0)),
        ],
        out_shape=[
            jax.ShapeDtypeStruct((m, HEADS * C_QK_PAD), BF16),
            jax.ShapeDtypeStruct((m, HEADS * C_QK_PAD), BF16),
            jax.ShapeDtypeStruct((m, HEADS * HEAD_DIM), BF16),
        ],
        compiler_params=_cparams(("parallel",)),
        name="mla_up",
    )(proj, proj, proj, posf, gq, gkv, fq, fk, wq, wqp, wk, wv)


def _fox_cumsum_kernel(f_ref, b_ref, o_ref, *, chunk):
    s_len = f_ref.shape[0]
    r = lax.broadcasted_iota(jnp.int32, (chunk, chunk), 0)
    c = lax.broadcasted_iota(jnp.int32, (chunk, chunk), 1)
    tri = (c <= r).astype(F32)
    carry = jnp.zeros((1, LANES), F32)
    for i in range(s_len // chunk):
        x = f_ref[i * chunk:(i + 1) * chunk, :].astype(F32) + b_ref[...]
        ls = jnp.minimum(x, 0.0) - jnp.log(1.0 + jnp.exp(-jnp.abs(x)))
        cum = jnp.dot(tri, ls, preferred_element_type=F32, precision=lax.Precision.HIGHEST) + carry
        o_ref[i * chunk:(i + 1) * chunk, :] = cum
        carry = cum[chunk - 1:chunk, :]


def fox_cumsum_call(proj3, bias_row):
    b, s_len, _ = proj3.shape
    chunk = min(512, s_len)
    return pl.pallas_call(
        functools.partial(_fox_cumsum_kernel, chunk=chunk),
        grid=(b,),
        in_specs=[pl.BlockSpec((None, s_len, LANES), lambda i: (i, 0, OFF_BF // LANES)),
                  pl.BlockSpec((1, LANES), lambda i: (0, 0))],
        out_specs=pl.BlockSpec((None, s_len, LANES), lambda i: (i, 0, 0)),
        out_shape=jax.ShapeDtypeStruct((b, s_len, LANES), F32),
        compiler_params=_cparams(("parallel",)),
        name="fox_cumsum",
    )(proj3, bias_row)


def _flash_loop(qt_scr, k_ref, v_ref, m_scr, l_scr, acc_scr, *, q0, tq, tk, bias_fn, causal_chunked):
    ncols = qt_scr.shape[1]
    m_scr[...] = jnp.full(m_scr.shape, NEG_INF, F32)
    l_scr[...] = jnp.zeros(l_scr.shape, F32)
    acc_scr[...] = jnp.zeros(acc_scr.shape, F32)

    def step(j, masked, c0=0):
        cols = slice(c0, ncols)
        w = ncols - c0
        ks = pl.multiple_of(j * tk, tk)
        s = jnp.dot(k_ref[pl.ds(ks, tk), :], qt_scr[:, cols], preferred_element_type=F32)
        if bias_fn is not None:
            s = bias_fn(s, ks, w)
        if masked:
            kidx = ks + lax.broadcasted_iota(jnp.int32, (tk, w), 0)
            qidx = q0 + ((c0 + lax.broadcasted_iota(jnp.int32, (tk, w), 1)) & (tq - 1))
            if causal_chunked:
                keep = (kidx >> CHUNK_SHIFT) <= (qidx >> CHUNK_SHIFT)
            else:
                keep = kidx <= qidx
            s = jnp.where(keep, s, NEG_INF)
        m_prev = m_scr[:, cols]
        m_new = jnp.maximum(m_prev, jnp.max(s, axis=0, keepdims=True))
        alpha = jnp.exp2(m_prev - m_new)
        p = jnp.exp2(s - m_new)
        l_scr[:, cols] = alpha * l_scr[:, cols] + jnp.sum(p, axis=0, keepdims=True)
        pv = lax.dot_general(v_ref[pl.ds(ks, tk), :], p.astype(BF16), (((0,), (0,)), ((), ())),
                             preferred_element_type=F32)
        acc_scr[:, cols] = alpha * acc_scr[:, cols] + pv
        m_scr[:, cols] = m_new

    n_full = q0 // tk

    def body(i, carry):
        step(2 * i, False)
        step(2 * i + 1, False)
        return carry

    lax.fori_loop(0, n_full // 2, body, 0)
    single_map = ncols == tq
    for d in range(tq // tk):
        step(n_full + d, True, d * tk if single_map else 0)


def _lane_tile(x, ncols):
    return x if ncols == x.shape[1] else jnp.concatenate([x] * (ncols // x.shape[1]), axis=1)


LOG2E = math.log2(math.e)


def _attn_a_kernel(slopes_ref, q_ref, k_ref, v_ref, posq_ref, posk_ref, lam_ref, g_ref, o_ref,
                   qt_scr, m_scr, l_scr, acc_scr, *, tq, tk, lam_init):
    h = pl.program_id(1)
    q0 = pl.program_id(2) * tq
    q_t = q_ref[...].astype(F32).T
    row = lax.broadcasted_iota(jnp.int32, q_t.shape, 0)
    zero = jnp.zeros_like(q_t)
    qt_scr[:, :tq] = jnp.where(row < A_QK_DIM, q_t, zero).astype(BF16)
    qt_scr[:, tq:] = jnp.where(row >= A_QK_DIM, q_t, zero).astype(BF16)
    slope = slopes_ref[h] * LOG2E
    pq = _lane_tile(posq_ref[...], 2 * tq)

    def bias_fn(s, ks, w):
        pk = _lane_tile(posk_ref[pl.ds(ks, tk), :], w)
        return s - slope * jnp.abs(pq[:, 2 * tq - w:] - pk)

    _flash_loop(qt_scr, k_ref, v_ref, m_scr, l_scr, acc_scr, q0=q0, tq=tq, tk=tk, bias_fn=bias_fn,
                causal_chunked=True)
    dl = lam_ref[...]
    lam = (jnp.exp(jnp.sum(dl[0:1] * dl[1:2], axis=1, keepdims=True))
           - jnp.exp(jnp.sum(dl[2:3] * dl[3:4], axis=1, keepdims=True)) + lam_init)
    accn = acc_scr[...] * (1.0 / l_scr[...])
    o = (accn[:, :tq] - lam * accn[:, tq:]).T
    o_ref[...] = (_rms(o, g_ref[...]) * (1.0 - lam_init)).astype(o_ref.dtype)


def attn_a_call(proj3, posq_row, posk_rep, slopes, diff_lambda, subln_g, lam_init, tq=512, tk=256):
    b, s_len, _ = proj3.shape
    tq = min(tq, s_len)
    assert tq % (2 * tk) == 0 and s_len % tq == 0
    kern = functools.partial(_attn_a_kernel, tq=tq, tk=tk, lam_init=lam_init)
    cb = OFF_A // HEAD_DIM
    return pl.pallas_call(
        kern,
        grid=(b, HEADS, s_len // tq),
        in_specs=[
            pl.BlockSpec(memory_space=pltpu.SMEM),
            pl.BlockSpec((None, tq, HEAD_DIM), lambda bi, h, i: (bi, i, cb + h)),
            pl.BlockSpec((None, s_len, HEAD_DIM), lambda bi, h, i: (bi, 0, cb + HEADS + h)),
            pl.BlockSpec((None, s_len, HEAD_DIM), lambda bi, h, i: (bi, 0, cb + 2 * HEADS + h)),
            pl.BlockSpec((None, 1, tq), lambda bi, h, i: (bi, 0, i)),
            pl.BlockSpec((None, s_len, LANES), lambda bi, h, i: (bi, 0, 0)),
            pl.BlockSpec((4, A_QK_DIM), lambda bi, h, i: (0, 0)),
            pl.BlockSpec((1, HEAD_DIM), lambda bi, h, i: (0, 0)),
        ],
        out_specs=pl.BlockSpec((None, tq, HEAD_DIM), lambda bi, h, i: (bi, i, h)),
        out_shape=jax.ShapeDtypeStruct((b, s_len, HEADS * HEAD_DIM), BF16),
        scratch_shapes=[pltpu.VMEM((HEAD_DIM, 2 * tq), BF16), pltpu.VMEM((1, 2 * tq), F32),
                        pltpu.VMEM((1, 2 * tq), F32), pltpu.VMEM((HEAD_DIM, 2 * tq), F32)],
        compiler_params=_cparams(("parallel", "parallel", "arbitrary")),
        name="attn_diff",
    )(slopes, proj3, proj3, proj3, posq_row, posk_rep, diff_lambda, subln_g.reshape(1, HEAD_DIM))


def _attn_b_kernel(q_ref, k_ref, v_ref, cum_ref, o_ref, col_scr, qt_scr, m_scr, l_scr, acc_scr, *, tq, tk):
    h = pl.program_id(1)
    qi = pl.program_id(2)

    @pl.when(qi == 0)
    def _():
        lane = lax.broadcasted_iota(jnp.int32, cum_ref.shape, 1)
        col = jnp.sum(jnp.where(lane == h, cum_ref[...], 0.0), axis=1, keepdims=True)
        col_scr[...] = jnp.broadcast_to(col * LOG2E, col_scr.shape)

    qt_scr[...] = q_ref[...].astype(F32).T.astype(BF16)

    def bias_fn(s, ks, w):
        return s - _lane_tile(col_scr[pl.ds(ks, tk), :], w)

    _flash_loop(qt_scr, k_ref, v_ref, m_scr, l_scr, acc_scr, q0=qi * tq, tq=tq, tk=tk, bias_fn=bias_fn,
                causal_chunked=False)
    o_ref[...] = (acc_scr[...] * (1.0 / l_scr[...])).T.astype(o_ref.dtype)


def attn_b_call(proj3, cum, tq=1024, tk=256):
    b, s_len, _ = proj3.shape
    tq = min(tq, s_len)
    assert tq % (2 * tk) == 0 and s_len % tq == 0
    cb = OFF_B // HEAD_DIM
    return pl.pallas_call(
        functools.partial(_attn_b_kernel, tq=tq, tk=tk),
        grid=(b, HEADS, s_len // tq),
        in_specs=[
            pl.BlockSpec((None, tq, HEAD_DIM), lambda bi, h, i: (bi, i, cb + h)),
            pl.BlockSpec((None, s_len, HEAD_DIM), lambda bi, h, i: (bi, 0, cb + HEADS + h)),
            pl.BlockSpec((None, s_len, HEAD_DIM), lambda bi, h, i: (bi, 0, cb + 2 * HEADS + h)),
            pl.BlockSpec((None, s_len, LANES), lambda bi, h, i: (bi, 0, 0)),
        ],
        out_specs=pl.BlockSpec((None, tq, HEAD_DIM), lambda bi, h, i: (bi, i, h)),
        out_shape=jax.ShapeDtypeStruct((b, s_len, HEADS * HEAD_DIM), BF16),
        scratch_shapes=[pltpu.VMEM((s_len, LANES), F32), pltpu.VMEM((HEAD_DIM, tq), BF16),
                        pltpu.VMEM((1, tq), F32), pltpu.VMEM((1, tq), F32), pltpu.VMEM((HEAD_DIM, tq), F32)],
        compiler_params=_cparams(("parallel", "parallel", "arbitrary")),
        name="attn_fox",
    )(proj3, proj3, proj3, cum)


def _attn_c_kernel(q_ref, k_ref, v_ref, o_ref, qt_scr, m_scr, l_scr, acc_scr, *, tq, tk):
    qt_scr[...] = q_ref[...].astype(F32).T.astype(BF16)
    _flash_loop(qt_scr, k_ref, v_ref, m_scr, l_scr, acc_scr, q0=pl.program_id(2) * tq, tq=tq, tk=tk,
                bias_fn=None, causal_chunked=True)
    o_ref[...] = (acc_scr[...] * (1.0 / l_scr[...])).T.astype(o_ref.dtype)


def attn_c_call(q3, k3, v3, tq=1024, tk=256):
    b, s_len, _ = q3.shape
    tq = min(tq, s_len)
    assert tq % (2 * tk) == 0 and s_len % tq == 0
    return pl.pallas_call(
        functools.partial(_attn_c_kernel, tq=tq, tk=tk),
        grid=(b, HEADS, s_len // tq),
        in_specs=[
            pl.BlockSpec((None, tq, C_QK_PAD), lambda bi, h, i: (bi, i, h)),
            pl.BlockSpec((None, s_len, C_QK_PAD), lambda bi, h, i: (bi, 0, h)),
            pl.BlockSpec((None, s_len, HEAD_DIM), lambda bi, h, i: (bi, 0, h)),
        ],
        out_specs=pl.BlockSpec((None, tq, HEAD_DIM), lambda bi, h, i: (bi, i, h)),
        out_shape=jax.ShapeDtypeStruct((b, s_len, HEADS * HEAD_DIM), BF16),
        scratch_shapes=[pltpu.VMEM((C_QK_PAD, tq), BF16), pltpu.VMEM((1, tq), F32), pltpu.VMEM((1, tq), F32),
                        pltpu.VMEM((HEAD_DIM, tq), F32)],
        compiler_params=_cparams(("parallel", "parallel", "arbitrary")),
        name="attn_mla",
    )(q3, k3, v3)


def _merge_kernel(x_ref, oa_ref, ob_ref, oc_ref, ga_ref, gb_ref, gc_ref, wb_ref, wo_ref, g_ref,
                  x_out_ref, h_out_ref):
    merged = None
    for gi, (o_ref, gate_ref) in enumerate(((oa_ref, ga_ref), (ob_ref, gb_ref), (oc_ref, gc_ref))):
        br = jnp.dot(o_ref[...], wb_ref[gi], preferred_element_type=F32)
        term = jax.nn.sigmoid(gate_ref[...].astype(F32)) * br
        merged = term if merged is None else merged + term
    x_new = x_ref[...] + jnp.dot(merged.astype(BF16), wo_ref[...], preferred_element_type=F32)
    x_out_ref[...] = x_new
    h_out_ref[...] = _rms(x_new, g_ref[...]).astype(h_out_ref.dtype)


def merge_call(x, o_a, o_b, o_c, proj, w_branch, w_out, norm_g, tm=256):
    m, d = x.shape
    tm = min(tm, m)
    row = lambda width: pl.BlockSpec((tm, width), lambda i: (i, 0))
    gate = lambda gi: pl.BlockSpec((tm, d), lambda i: (i, OFF_GATES // d + gi))
    return pl.pallas_call(
        _merge_kernel,
        grid=(m // tm,),
        in_specs=[
            row(d), row(BRANCH_WIDTH), row(BRANCH_WIDTH), row(BRANCH_WIDTH), gate(0), gate(1), gate(2),
            pl.BlockSpec(w_branch.shape, lambda i: (0, 0, 0), pipeline_mode=pl.Buffered(1)),
            pl.BlockSpec(w_out.shape, lambda i: (0, 0), pipeline_mode=pl.Buffered(1)),
            pl.BlockSpec((1, d), lambda i: (0, 0)),
        ],
        out_specs=[row(d), row(d)],
        out_shape=[jax.ShapeDtypeStruct((m, d), F32), jax.ShapeDtypeStruct((m, d), BF16)],
        compiler_params=_cparams(("parallel",)),
        name="merge_out",
    )(x, o_a, o_b, o_c, proj, proj, proj, w_branch, w_out, norm_g.reshape(1, d))


def _ffn_kernel(h_ref, x_ref, w1_ref, w3_ref, w2_ref, g_ref, *outs, emit_x):
    acc_ref = outs[-1]
    f = pl.program_id(1)

    @pl.when(f == 0)
    def _():
        acc_ref[...] = x_ref[...]

    h = h_ref[...]
    a = jnp.dot(h, w1_ref[...], preferred_element_type=F32)
    b = jnp.dot(h, w3_ref[...], preferred_element_type=F32)
    act = (a * jax.nn.sigmoid(a) * b).astype(BF16)
    acc_ref[...] += jnp.dot(act, w2_ref[...], preferred_element_type=F32)

    @pl.when(f == pl.num_programs(1) - 1)
    def _():
        x_new = acc_ref[...]
        if emit_x:
            outs[0][...] = x_new
        outs[-2][...] = _rms(x_new, g_ref[...]).astype(outs[-2].dtype)


def ffn_call(h, x, w1, w3, w2, norm_g, *, emit_x, norm_dtype, tm=512, tf=512):
    m, d = x.shape
    tm = min(tm, m)
    row = pl.BlockSpec((tm, d), lambda i, f: (i, 0))
    w_in = pl.BlockSpec((d, tf), lambda i, f: (0, f))
    out_specs = [row]
    out_shape = [jax.ShapeDtypeStruct((m, d), norm_dtype)]
    if emit_x:
        out_specs = [row, row]
        out_shape = [jax.ShapeDtypeStruct((m, d), F32)] + out_shape
    return pl.pallas_call(
        functools.partial(_ffn_kernel, emit_x=emit_x),
        grid=(m // tm, w1.shape[1] // tf),
        in_specs=[row, row, w_in, w_in, pl.BlockSpec((tf, d), lambda i, f: (f, 0)),
                  pl.BlockSpec((1, d), lambda i, f: (0, 0))],
        out_specs=out_specs,
        out_shape=out_shape,
        scratch_shapes=[pltpu.VMEM((tm, d), F32)],
        compiler_params=_cparams(("parallel", "arbitrary")),
        name="dense_ffn",
    )(h, x, w1, w3, w2, norm_g.reshape(1, d))


def _router_kernel(x_ref, g_ref, w_ref, i_ref, c_ref):
    h = _rms(x_ref[...], g_ref[...])
    logits = jnp.dot(h, w_ref[...], preferred_element_type=F32, precision=lax.Precision.HIGHEST)
    lane = lax.broadcasted_iota(jnp.int32, logits.shape, 1)
    logits = jnp.where(lane < N_EXPERTS, logits, -jnp.inf)
    m1 = jnp.max(logits, axis=1, keepdims=True)
    i1 = jnp.min(jnp.where(logits == m1, lane, LANES), axis=1, keepdims=True)
    rest = jnp.where(lane == i1, -jnp.inf, logits)
    m2 = jnp.max(rest, axis=1, keepdims=True)
    i2 = jnp.min(jnp.where(rest == m2, lane, LANES), axis=1, keepdims=True)
    e2 = jnp.exp(m2 - m1)
    inv = 1.0 / (1.0 + e2)
    i_ref[...] = jnp.where(lane == 0, i1, jnp.where(lane == 1, i2, 0))
    c_ref[...] = jnp.where(lane == 0, inv, jnp.where(lane == 1, e2 * inv, 0.0))


def router_call(x, norm_g, w_pad, tm=512):
    m, d = x.shape
    tm = min(tm, m)
    out = pl.BlockSpec((tm, LANES), lambda i: (i, 0))
    return pl.pallas_call(
        _router_kernel,
        grid=(m // tm,),
        in_specs=[pl.BlockSpec((tm, d), lambda i: (i, 0)), pl.BlockSpec((1, d), lambda i: (0, 0)),
                  pl.BlockSpec((d, LANES), lambda i: (0, 0))],
        out_specs=[out, out],
        out_shape=[jax.ShapeDtypeStruct((m, LANES), jnp.int32), jax.ShapeDtypeStruct((m, LANES), F32)],
        compiler_params=_cparams(("parallel",)),
        name="router",
    )(x, norm_g.reshape(1, d), w_pad)


def _row_copy(src_hbm, idx_ref, dst_ref, sem, r):
    return pltpu.make_async_copy(src_hbm.at[pl.ds(idx_ref[0, r], 1)], dst_ref.at[pl.ds(r, 1)], sem)


def _gather_rows(src_hbm, idx_ref, dst_ref, sem):
    n = dst_ref.shape[0]

    def start(r, c):
        _row_copy(src_hbm, idx_ref, dst_ref, sem, r).start()
        return c

    def wait(r, c):
        _row_copy(src_hbm, idx_ref, dst_ref, sem, r).wait()
        return c

    lax.fori_loop(0, n, start, 0, unroll=8)
    lax.fori_loop(0, n, wait, 0, unroll=8)


def _gather_kernel(idx_ref, src_hbm, o_ref, sem):
    _gather_rows(src_hbm, idx_ref, o_ref, sem)


def gather_rows_call(src, idx, tg=256):
    n_rows = idx.shape[0]
    width = src.shape[1]
    return pl.pallas_call(
        _gather_kernel,
        grid=(n_rows // tg,),
        in_specs=[pl.BlockSpec((None, 1, tg), lambda i: (i, 0, 0), memory_space=pltpu.SMEM),
                  pl.BlockSpec(memory_space=pl.ANY)],
        out_specs=pl.BlockSpec((tg, width), lambda i: (i, 0)),
        out_shape=jax.ShapeDtypeStruct((n_rows, width), src.dtype),
        scratch_shapes=[pltpu.SemaphoreType.DMA],
        compiler_params=_cparams(("arbitrary",)),
        name="moe_gather",
    )(idx.reshape(n_rows // tg, 1, tg), src)


def _moe_group_kernel(te_ref, nv_ref, xs_ref, g_ref, w1_ref, w3_ref, w2_ref, o_ref, h_scr, acc_scr):
    del te_ref
    f = pl.program_id(1)
    valid = pl.program_id(0) < nv_ref[0]
    last = f == pl.num_programs(1) - 1

    @pl.when(valid & (f == 0))
    def _():
        h_scr[...] = _rms(xs_ref[...], g_ref[...]).astype(h_scr.dtype)
        acc_scr[...] = jnp.zeros(acc_scr.shape, F32)

    @pl.when(valid)
    def _():
        h = h_scr[...]
        a = jnp.dot(h, w1_ref[...], preferred_element_type=F32)
        b = jnp.dot(h, w3_ref[...], preferred_element_type=F32)
        act = (a * jax.nn.sigmoid(a) * b).astype(BF16)
        acc_scr[...] += jnp.dot(act, w2_ref[...], preferred_element_type=F32)

    @pl.when(valid & last)
    def _():
        o_ref[...] = acc_scr[...]

    @pl.when(jnp.logical_not(valid) & last)
    def _():
        o_ref[...] = jnp.zeros(o_ref.shape, o_ref.dtype)


def moe_group_call(xs, norm_g, w1, w3, w2, tile_expert, n_valid, tm, tf=512):
    mp, d = xs.shape
    n_f = w1.shape[-1] // tf

    def f_sel(i, f, nv):
        return jnp.where(i < nv[0], f, n_f - 1)

    w_in = pl.BlockSpec((None, d, tf), lambda i, f, te, nv: (te[i], 0, f_sel(i, f, nv)))
    w_out = pl.BlockSpec((None, tf, d), lambda i, f, te, nv: (te[i], f_sel(i, f, nv), 0))
    row = pl.BlockSpec((tm, d), lambda i, f, te, nv: (i, 0))
    return pl.pallas_call(
        _moe_group_kernel,
        grid_spec=pltpu.PrefetchScalarGridSpec(
            num_scalar_prefetch=2,
            grid=(mp // tm, n_f),
            in_specs=[row, pl.BlockSpec((1, d), lambda i, f, te, nv: (0, 0)), w_in, w_in, w_out],
            out_specs=row,
            scratch_shapes=[pltpu.VMEM((tm, d), BF16), pltpu.VMEM((tm, d), F32)],
        ),
        out_shape=jax.ShapeDtypeStruct((mp, d), F32),
        compiler_params=_cparams(("arbitrary", "arbitrary")),
        name="moe_experts",
    )(tile_expert, n_valid, xs, norm_g.reshape(1, d), w1, w3, w2)


def _moe_combine_kernel(p0_ref, p1_ref, x_ref, c_ref, g_ref, ys_hbm, *rest, emit_x):
    *outs, y0_scr, y1_scr, sem0, sem1 = rest
    n = x_ref.shape[0]

    def start(r, c):
        _row_copy(ys_hbm, p0_ref, y0_scr, sem0, r).start()
        _row_copy(ys_hbm, p1_ref, y1_scr, sem1, r).start()
        return c

    def wait(r, c):
        _row_copy(ys_hbm, p0_ref, y0_scr, sem0, r).wait()
        _row_copy(ys_hbm, p1_ref, y1_scr, sem1, r).wait()
        return c

    lax.fori_loop(0, n, start, 0, unroll=8)
    lax.fori_loop(0, n, wait, 0, unroll=8)
    lane = lax.broadcasted_iota(jnp.int32, c_ref.shape, 1)
    c = c_ref[...]
    c0 = jnp.sum(jnp.where(lane == 0, c, 0.0), axis=1, keepdims=True)
    c1 = jnp.sum(jnp.where(lane == 1, c, 0.0), axis=1, keepdims=True)
    x_new = x_ref[...] + c0 * y0_scr[...] + c1 * y1_scr[...]
    if emit_x:
        outs[0][...] = x_new
    outs[-1][...] = _rms(x_new, g_ref[...]).astype(outs[-1].dtype)


def moe_combine_call(x, weights, ys, pos0, pos1, norm_g, *, emit_x, norm_dtype, tg=256):
    m, d = x.shape
    tg = min(tg, m)
    idx = pl.BlockSpec((None, 1, tg), lambda i: (i, 0, 0), memory_space=pltpu.SMEM)
    row = pl.BlockSpec((tg, d), lambda i: (i, 0))
    out_specs = [row]
    out_shape = [jax.ShapeDtypeStruct((m, d), norm_dtype)]
    if emit_x:
        out_specs = [row, row]
        out_shape = [jax.ShapeDtypeStruct((m, d), F32)] + out_shape
    return pl.pallas_call(
        functools.partial(_moe_combine_kernel, emit_x=emit_x),
        grid=(m // tg,),
        in_specs=[idx, idx, row, pl.BlockSpec((tg, LANES), lambda i: (i, 0)),
                  pl.BlockSpec((1, d), lambda i: (0, 0)), pl.BlockSpec(memory_space=pl.ANY)],
        out_specs=out_specs,
        out_shape=out_shape,
        scratch_shapes=[pltpu.VMEM((tg, d), F32), pltpu.VMEM((tg, d), F32),
                        pltpu.SemaphoreType.DMA, pltpu.SemaphoreType.DMA],
        compiler_params=_cparams(("arbitrary",)),
        name="moe_combine",
    )(pos0.reshape(m // tg, 1, tg), pos1.reshape(m // tg, 1, tg), x, weights, norm_g.reshape(1, d), ys)


def _route(choice, tm):
    t = choice.shape[0]
    n_tiles = (TOP_K * t) // tm + N_EXPERTS
    e = choice.reshape(-1)
    onehot = (e[:, None] == jnp.arange(N_EXPERTS, dtype=jnp.int32)[None, :]).astype(jnp.int32)
    rank = jnp.cumsum(onehot, axis=0) - onehot
    counts = jnp.sum(onehot, axis=0)
    tiles_per = (counts + tm - 1) // tm
    tile_end = jnp.cumsum(tiles_per)
    start_row = (tile_end - tiles_per) * tm
    pos = jnp.sum(onehot * (start_row[None, :] + rank), axis=1)
    src = jnp.zeros((n_tiles * tm,), jnp.int32).at[pos].set(jnp.arange(TOP_K * t, dtype=jnp.int32) // TOP_K)
    n_valid = tile_end[-1:]
    tile_id = jnp.minimum(jnp.arange(n_tiles, dtype=jnp.int32), n_valid[0] - 1)
    tile_expert = jnp.sum((tile_id[:, None] >= tile_end[None, :]).astype(jnp.int32), axis=1)
    return pos.reshape(t, TOP_K), src, tile_expert.astype(jnp.int32), n_valid.astype(jnp.int32)


def _rot_pair(w):
    half = w.shape[-1] // 2
    return jnp.concatenate([-w[..., half:], w[..., :half]], axis=-1)


def _pack_in_proj(w):
    d = w.shape[0]
    bw = BRANCH_WIDTH
    aq, ak, av, bq, bk, bv = (w[:, i * bw:(i + 1) * bw] for i in range(6))
    o = 6 * bw
    bf = w[:, o:o + HEADS]
    o += HEADS
    cq = w[:, o:o + C_RANK]
    ckv = w[:, o + C_RANK:o + 2 * C_RANK]
    o += 2 * C_RANK
    ckr = w[:, o:o + C_ROPE_DIM]
    gates = w[:, o + C_ROPE_DIM:]
    z = lambda n: jnp.zeros((d, n), w.dtype)
    packed = jnp.concatenate(
        [gates, aq * (A_QK_DIM ** -0.5 * LOG2E), ak, av, bq * (HEAD_DIM ** -0.5 * LOG2E), bk, bv, cq, ckv,
         ckr, z(LANES - C_ROPE_DIM), _rot_pair(ckr), z(LANES - C_ROPE_DIM),
         bf, z(LANES - HEADS), z(N_PROJ - OFF_BF - LANES)], axis=1)
    return packed.astype(BF16)


def _pack_mla(w_uq, w_uk, w_uv):
    r = w_uq.shape[0]
    scale = (C_NOPE_DIM + C_ROPE_DIM) ** -0.5 * LOG2E
    nope, rope = w_uq[..., :C_NOPE_DIM] * scale, w_uq[..., C_NOPE_DIM:] * scale
    zpad = jnp.zeros((r, HEADS, C_QK_PAD - C_NOPE_DIM - C_ROPE_DIM), w_uq.dtype)
    wq = jnp.concatenate([nope, rope, zpad], axis=-1).reshape(r, HEADS * C_QK_PAD)
    wqp = jnp.concatenate([jnp.zeros_like(nope), _rot_pair(rope), zpad], axis=-1).reshape(r, HEADS * C_QK_PAD)
    return (wq.astype(BF16), wqp.astype(BF16), w_uk.reshape(r, -1).astype(BF16),
            w_uv.reshape(r, -1).astype(BF16))


def _rope_freqs():
    half = C_ROPE_DIM // 2
    inv = ROPE_THETA ** (-jnp.arange(half, dtype=F32) / half)
    fk = jnp.concatenate([inv, inv, jnp.zeros((LANES - C_ROPE_DIM,), F32)])
    fq = jnp.concatenate([jnp.zeros((C_NOPE_DIM,), F32), fk])
    return fq.reshape(1, C_QK_PAD), fk.reshape(1, LANES)


def kernel(x, positions, attn_norm_g, w_in, diff_lambda, diff_subln_g, fox_forget_bias, mla_q_norm_g, mla_kv_norm_g, mla_w_uq, mla_w_uk, mla_w_uv, w_branch, w_out, ffn_norm_g, dense_w1, dense_w3, dense_w2, router_w, expert_w1, expert_w3, expert_w2, final_norm_g):
    b, s_len, d = x.shape
    m = b * s_len
    depth = w_in.shape[0]
    posf = positions.astype(F32)
    posq_row = posf.reshape(b, 1, s_len)
    posk_rep = jnp.broadcast_to(posf[:, :, None], (b, s_len, LANES))
    pos_col = posf.reshape(m, 1)
    slopes = jnp.exp2(-8.0 * (jnp.arange(HEADS, dtype=F32) + 1.0) / HEADS)
    fq, fk = _rope_freqs()

    xr = x.reshape(m, d)
    hn = rmsnorm_call(xr, attn_norm_g[0], BF16)
    out = None
    for l in range(depth):
        lam_init = 0.8 - 0.6 * math.exp(-0.3 * l)
        proj = matmul_call(hn, _pack_in_proj(w_in[l]), BF16)
        proj3 = proj.reshape(b, s_len, N_PROJ)

        o_a = attn_a_call(proj3, posq_row, posk_rep, slopes, diff_lambda[l], diff_subln_g[l], lam_init)
        bias_row = jnp.zeros((1, LANES), F32).at[0, :HEADS].set(fox_forget_bias[l])
        o_b = attn_b_call(proj3, fox_cumsum_call(proj3, bias_row))
        wq, wqp, wk, wv = _pack_mla(mla_w_uq[l], mla_w_uk[l], mla_w_uv[l])
        q_c, k_c, v_c = mla_up_call(proj, pos_col, mla_q_norm_g[l].reshape(1, -1),
                                    mla_kv_norm_g[l].reshape(1, -1), fq, fk, wq, wqp, wk, wv)
        o_c = attn_c_call(q_c.reshape(b, s_len, -1), k_c.reshape(b, s_len, -1), v_c.reshape(b, s_len, -1))

        x1, h_ffn = merge_call(xr, o_a.reshape(m, -1), o_b.reshape(m, -1), o_c.reshape(m, -1), proj,
                               w_branch[l].astype(BF16), w_out[l].astype(BF16), ffn_norm_g[l])
        last = l == depth - 1
        next_g = final_norm_g if last else attn_norm_g[l + 1]
        j = l // 2
        if l % 2 == 0:
            res = ffn_call(h_ffn, x1, dense_w1[j].astype(BF16), dense_w3[j].astype(BF16),
                           dense_w2[j].astype(BF16), next_g, emit_x=not last,
                           norm_dtype=F32 if last else BF16)
        else:
            w_pad = jnp.zeros((d, LANES), F32).at[:, :N_EXPERTS].set(router_w[j])
            choice, weights = router_call(x1, ffn_norm_g[l], w_pad)
            tm_moe = min(512, m)
            pos, src, tile_expert, n_valid = _route(choice[:, :TOP_K], tm_moe)
            xs = gather_rows_call(x1, src)
            ys = moe_group_call(xs, ffn_norm_g[l], expert_w1[j].astype(BF16), expert_w3[j].astype(BF16),
                                expert_w2[j].astype(BF16), tile_expert, n_valid, tm_moe)
            res = moe_combine_call(x1, weights, ys, pos[:, 0], pos[:, 1], next_g, emit_x=not last,
                                   norm_dtype=F32 if last else BF16)
        if last:
            out = res[0]
        else:
            xr, hn = res
    return out.reshape(b, s_len, d)
```

```python
import functools
import math

import jax
import jax.numpy as jnp
from jax import lax
from jax.experimental import pallas as pl
from jax.experimental.pallas import tpu as pltpu

F32 = jnp.float32
BF16 = jnp.bfloat16

D_MODEL = 2048
CHUNK_SHIFT = 6
NORM_EPS = 1e-6
NEG_INF = -1e30

HEADS = 8
HEAD_DIM = 128
A_QK_DIM = 64
C_RANK = 512
C_NOPE_DIM = 128
C_ROPE_DIM = 64
C_QK_PAD = 256
ROPE_THETA = 10000.0
N_BRANCHES = 3
BRANCH_WIDTH = 1024
N_EXPERTS = 8
TOP_K = 2
LANES = 128

OFF_GATES = 0
OFF_A = 3 * D_MODEL
OFF_B = OFF_A + 3 * BRANCH_WIDTH
OFF_CQ = OFF_B + 3 * BRANCH_WIDTH
OFF_CKV = OFF_CQ + C_RANK
OFF_CKR = OFF_CKV + C_RANK
OFF_BF = OFF_CKR + 2 * LANES
N_PROJ = 13824

VMEM_LIMIT = 56 * 1024 * 1024


def _cparams(sem):
    return pltpu.CompilerParams(dimension_semantics=sem, vmem_limit_bytes=VMEM_LIMIT)


def _rms(x, g):
    return x * lax.rsqrt(jnp.mean(x * x, axis=-1, keepdims=True) + NORM_EPS) * g


def _rmsnorm_kernel(x_ref, g_ref, o_ref):
    o_ref[...] = _rms(x_ref[...].astype(F32), g_ref[...]).astype(o_ref.dtype)


def rmsnorm_call(x, g, out_dtype, tm=512):
    m, d = x.shape
    tm = min(tm, m)
    return pl.pallas_call(
        _rmsnorm_kernel,
        grid=(m // tm,),
        in_specs=[pl.BlockSpec((tm, d), lambda i: (i, 0)), pl.BlockSpec((1, d), lambda i: (0, 0))],
        out_specs=pl.BlockSpec((tm, d), lambda i: (i, 0)),
        out_shape=jax.ShapeDtypeStruct((m, d), out_dtype),
        compiler_params=_cparams(("parallel",)),
        name="rmsnorm",
    )(x, g.reshape(1, d))


def _matmul_kernel(a_ref, b_ref, o_ref):
    o_ref[...] = jnp.dot(a_ref[...], b_ref[...], preferred_element_type=F32).astype(o_ref.dtype)


def matmul_call(a, b, out_dtype, tm=1024, tn=512):
    m, k = a.shape
    _, n = b.shape
    tm = min(tm, m)
    return pl.pallas_call(
        _matmul_kernel,
        grid=(m // tm, n // tn),
        in_specs=[pl.BlockSpec((tm, k), lambda i, j: (i, 0)), pl.BlockSpec((k, tn), lambda i, j: (0, j))],
        out_specs=pl.BlockSpec((tm, tn), lambda i, j: (i, j)),
        out_shape=jax.ShapeDtypeStruct((m, n), out_dtype),
        compiler_params=_cparams(("parallel", "arbitrary")),
        name="in_proj",
    )(a, b)


def _mla_up_kernel(cq_ref, ckv_ref, ckr_ref, pos_ref, gq_ref, gkv_ref, f_ref,
                   wq_ref, wqp_ref, wk_ref, wv_ref, q_ref, k_ref, v_ref):
    cqn = _rms(cq_ref[...].astype(F32), gq_ref[...]).astype(BF16)
    ckvn = _rms(ckv_ref[...].astype(F32), gkv_ref[...]).astype(BF16)
    ang = pos_ref[...] * f_ref[...]
    cos, sin = jnp.cos(ang), jnp.sin(ang)
    q = jnp.dot(cqn, wq_ref[...], preferred_element_type=F32)
    qp = jnp.dot(cqn, wqp_ref[...], preferred_element_type=F32)
    for h in range(HEADS):
        nope = slice(h * C_QK_PAD, h * C_QK_PAD + C_NOPE_DIM)
        rope = slice(h * C_QK_PAD + C_NOPE_DIM, (h + 1) * C_QK_PAD)
        q_ref[:, nope] = q[:, nope].astype(q_ref.dtype)
        q_ref[:, rope] = (q[:, rope] * cos + qp[:, h * LANES:(h + 1) * LANES] * sin).astype(q_ref.dtype)
    ckr = ckr_ref[...].astype(F32)
    kr = (ckr[:, :LANES] * cos + ckr[:, LANES:] * sin).astype(k_ref.dtype)
    kn = jnp.dot(ckvn, wk_ref[...], preferred_element_type=F32).astype(k_ref.dtype)
    for h in range(HEADS):
        k_ref[:, h * C_QK_PAD:h * C_QK_PAD + C_NOPE_DIM] = kn[:, h * C_NOPE_DIM:(h + 1) * C_NOPE_DIM]
        k_ref[:, h * C_QK_PAD + C_NOPE_DIM:(h + 1) * C_QK_PAD] = kr
    v_ref[...] = jnp.dot(ckvn, wv_ref[...], preferred_element_type=F32).astype(v_ref.dtype)


def mla_up_call(proj, posf, gq, gkv, freqs, wq, wqp, wk, wv, tm=512):
    m = proj.shape[0]
    tm = min(tm, m)
    const = lambda shape: pl.BlockSpec(shape, lambda i: (0, 0))
    return pl.pallas_call(
        _mla_up_kernel,
        grid=(m // tm,),
        in_specs=[
            pl.BlockSpec((tm, C_RANK), lambda i: (i, OFF_CQ // C_RANK)),
            pl.BlockSpec((tm, C_RANK), lambda i: (i, OFF_CKV // C_RANK)),
            pl.BlockSpec((tm, 2 * LANES), lambda i: (i, OFF_CKR // (2 * LANES))),
            pl.BlockSpec((tm, 1), lambda i: (i, 0)),
            const((1, C_RANK)), const((1, C_RANK)), const((1, LANES)),
            const(wq.shape), const(wqp.shape), const(wk.shape), const(wv.shape),
        ],
        out_specs=[
            pl.BlockSpec((tm, HEADS * C_QK_PAD), lambda i: (i, 0)),
            pl.BlockSpec((tm, HEADS * C_QK_PAD), lambda i: (i, 0)),
            pl.BlockSpec((tm, HEADS * HEAD_DIM), lambda i: (i, 0)),
        ],
        out_shape=[
            jax.ShapeDtypeStruct((m, HEADS * C_QK_PAD), BF16),
            jax.ShapeDtypeStruct((m, HEADS * C_QK_PAD), BF16),
            jax.ShapeDtypeStruct((m, HEADS * HEAD_DIM), BF16),
        ],
        compiler_params=_cparams(("parallel",)),
        name="mla_up",
    )(proj, proj, proj, posf, gq, gkv, freqs, wq, wqp, wk, wv)


def _fox_cumsum_kernel(f_ref, b_ref, o_ref, *, chunk):
    s_len = f_ref.shape[0]
    r = lax.broadcasted_iota(jnp.int32, (chunk, chunk), 0)
    c = lax.broadcasted_iota(jnp.int32, (chunk, chunk), 1)
    tri = (c <= r).astype(F32)
    carry = jnp.zeros((1, LANES), F32)
    for i in range(s_len // chunk):
        x = f_ref[i * chunk:(i + 1) * chunk, :].astype(F32) + b_ref[...]
        ls = jnp.minimum(x, 0.0) - jnp.log(1.0 + jnp.exp(-jnp.abs(x)))
        cum = jnp.dot(tri, ls, preferred_element_type=F32, precision=lax.Precision.HIGHEST) + carry
        o_ref[i * chunk:(i + 1) * chunk, :] = cum
        carry = cum[chunk - 1:chunk, :]


def fox_cumsum_call(proj3, bias_row):
    b, s_len, _ = proj3.shape
    chunk = min(512, s_len)
    return pl.pallas_call(
        functools.partial(_fox_cumsum_kernel, chunk=chunk),
        grid=(b,),
        in_specs=[pl.BlockSpec((None, s_len, LANES), lambda i: (i, 0, OFF_BF // LANES)),
                  pl.BlockSpec((1, LANES), lambda i: (0, 0))],
        out_specs=pl.BlockSpec((None, s_len, LANES), lambda i: (i, 0, 0)),
        out_shape=jax.ShapeDtypeStruct((b, s_len, LANES), F32),
        compiler_params=_cparams(("parallel",)),
        name="fox_cumsum",
    )(proj3, bias_row)


def _flash_loop(qt_scr, k_ref, v_ref, m_scr, l_scr, acc_scr, *, q0, tq, tk, bias_fn, causal_chunked):
    ncols = qt_scr.shape[1]
    m_scr[...] = jnp.full(m_scr.shape, NEG_INF, F32)
    l_scr[...] = jnp.zeros(l_scr.shape, F32)
    acc_scr[...] = jnp.zeros(acc_scr.shape, F32)

    def scores(j, masked, c0):
        w = ncols - c0
        ks = pl.multiple_of(j * tk, tk)
        s = jnp.dot(k_ref[pl.ds(ks, tk), :], qt_scr[:, c0:], preferred_element_type=F32)
        if bias_fn is not None:
            s = bias_fn(s, ks, c0, w)
        if masked:
            kidx = ks + lax.broadcasted_iota(jnp.int32, (tk, w), 0)
            qidx = q0 + ((c0 + lax.broadcasted_iota(jnp.int32, (tk, w), 1)) & (tq - 1))
            if causal_chunked:
                keep = (kidx >> CHUNK_SHIFT) <= (qidx >> CHUNK_SHIFT)
            else:
                keep = kidx <= qidx
            s = jnp.where(keep, s, NEG_INF)
        return s

    def update(s, j, c0):
        cols = slice(c0, ncols)
        ks = pl.multiple_of(j * tk, tk)
        m_prev = m_scr[:, cols]
        m_new = jnp.maximum(m_prev, jnp.max(s, axis=0, keepdims=True))
        alpha = jnp.exp2(m_prev - m_new)
        p = jnp.exp2(s - m_new)
        l_scr[:, cols] = alpha * l_scr[:, cols] + jnp.sum(p, axis=0, keepdims=True)
        pv = lax.dot_general(v_ref[pl.ds(ks, tk), :], p.astype(BF16), (((0,), (0,)), ((), ())),
                             preferred_element_type=F32)
        acc_scr[:, cols] = alpha * acc_scr[:, cols] + pv
        m_scr[:, cols] = m_new

    def steps(tiles):
        s_all = [scores(j, masked, c0) for j, masked, c0 in tiles]
        for s, (j, _, c0) in zip(s_all, tiles):
            update(s, j, c0)

    n_full = q0 // tk

    def body(i, carry):
        steps([(2 * i, False, 0), (2 * i + 1, False, 0)])
        return carry

    lax.fori_loop(0, n_full // 2, body, 0)
    if (tq // tk) % 2:

        @pl.when(n_full % 2 == 1)
        def _():
            steps([(n_full - 1, False, 0)])

    single_map = ncols == tq
    diag = [(n_full + d, True, d * tk if single_map else 0) for d in range(tq // tk)]
    for i in range(0, len(diag), 2):
        steps(diag[i:i + 2])


def _lane_tile(x, ncols):
    return x if ncols == x.shape[1] else jnp.concatenate([x] * (ncols // x.shape[1]), axis=1)


LOG2E = math.log2(math.e)


def _attn_a_kernel(slopes_ref, q_ref, k_ref, v_ref, posq_ref, posk_ref, lam_ref, g_ref, o_ref,
                   qt_scr, m_scr, l_scr, acc_scr, *, tq, tk, lam_init):
    h = pl.program_id(1)
    q0 = pl.program_id(2) * tq
    q_t = q_ref[...].astype(F32).T
    row = lax.broadcasted_iota(jnp.int32, q_t.shape, 0)
    zero = jnp.zeros_like(q_t)
    qt_scr[:, :tq] = jnp.where(row < A_QK_DIM, q_t, zero).astype(BF16)
    qt_scr[:, tq:] = jnp.where(row >= A_QK_DIM, q_t, zero).astype(BF16)
    slope = slopes_ref[h] * LOG2E
    pq = _lane_tile(posq_ref[...], 2 * tq)

    def bias_fn(s, ks, c0, w):
        pk = _lane_tile(posk_ref[pl.ds(ks, tk), :], w)
        return s - slope * jnp.abs(pq[:, c0:c0 + w] - pk)

    _flash_loop(qt_scr, k_ref, v_ref, m_scr, l_scr, acc_scr, q0=q0, tq=tq, tk=tk, bias_fn=bias_fn,
                causal_chunked=True)
    dl = lam_ref[...]
    lam = (jnp.exp(jnp.sum(dl[0:1] * dl[1:2], axis=1, keepdims=True))
           - jnp.exp(jnp.sum(dl[2:3] * dl[3:4], axis=1, keepdims=True)) + lam_init)
    accn = acc_scr[...] * (1.0 / l_scr[...])
    o = (accn[:, :tq] - lam * accn[:, tq:]).T
    o_ref[...] = (_rms(o, g_ref[...]) * (1.0 - lam_init)).astype(o_ref.dtype)


def attn_a_call(proj3, posq_row, posk_rep, slopes, diff_lambda, subln_g, lam_init, tq=512, tk=512):
    b, s_len, _ = proj3.shape
    tq = min(tq, s_len)
    tk = min(tk, tq)
    assert tq % tk == 0 and s_len % tq == 0
    kern = functools.partial(_attn_a_kernel, tq=tq, tk=tk, lam_init=lam_init)
    cb = OFF_A // HEAD_DIM
    return pl.pallas_call(
        kern,
        grid=(b, HEADS, s_len // tq),
        in_specs=[
            pl.BlockSpec(memory_space=pltpu.SMEM),
            pl.BlockSpec((None, tq, HEAD_DIM), lambda bi, h, i: (bi, i, cb + h)),
            pl.BlockSpec((None, s_len, HEAD_DIM), lambda bi, h, i: (bi, 0, cb + HEADS + h)),
            pl.BlockSpec((None, s_len, HEAD_DIM), lambda bi, h, i: (bi, 0, cb + 2 * HEADS + h)),
            pl.BlockSpec((None, 1, tq), lambda bi, h, i: (bi, 0, i)),
            pl.BlockSpec((None, s_len, LANES), lambda bi, h, i: (bi, 0, 0)),
            pl.BlockSpec((4, A_QK_DIM), lambda bi, h, i: (0, 0)),
            pl.BlockSpec((1, HEAD_DIM), lambda bi, h, i: (0, 0)),
        ],
        out_specs=pl.BlockSpec((None, tq, HEAD_DIM), lambda bi, h, i: (bi, i, h)),
        out_shape=jax.ShapeDtypeStruct((b, s_len, HEADS * HEAD_DIM), BF16),
        scratch_shapes=[pltpu.VMEM((HEAD_DIM, 2 * tq), BF16), pltpu.VMEM((1, 2 * tq), F32),
                        pltpu.VMEM((1, 2 * tq), F32), pltpu.VMEM((HEAD_DIM, 2 * tq), F32)],
        compiler_params=_cparams(("parallel", "parallel", "arbitrary")),
        name="attn_diff",
    )(slopes, proj3, proj3, proj3, posq_row, posk_rep, diff_lambda, subln_g.reshape(1, HEAD_DIM))


def _attn_b_kernel(q_ref, k_ref, v_ref, cum_ref, o_ref, col_scr, qt_scr, m_scr, l_scr, acc_scr, *, tq, tk):
    h = pl.program_id(1)
    qi = pl.program_id(2)

    @pl.when(qi == 0)
    def _():
        lane = lax.broadcasted_iota(jnp.int32, cum_ref.shape, 1)
        col = jnp.sum(jnp.where(lane == h, cum_ref[...], 0.0), axis=1, keepdims=True)
        col_scr[...] = jnp.broadcast_to(col * LOG2E, col_scr.shape)

    qt_scr[...] = q_ref[...].astype(F32).T.astype(BF16)

    def bias_fn(s, ks, c0, w):
        return s - _lane_tile(col_scr[pl.ds(ks, tk), :], w)

    _flash_loop(qt_scr, k_ref, v_ref, m_scr, l_scr, acc_scr, q0=qi * tq, tq=tq, tk=tk, bias_fn=bias_fn,
                causal_chunked=False)
    o_ref[...] = (acc_scr[...] * (1.0 / l_scr[...])).T.astype(o_ref.dtype)


def attn_b_call(proj3, cum, tq=1024, tk=512):
    b, s_len, _ = proj3.shape
    tq = min(tq, s_len)
    tk = min(tk, tq)
    assert tq % tk == 0 and s_len % tq == 0
    cb = OFF_B // HEAD_DIM
    return pl.pallas_call(
        functools.partial(_attn_b_kernel, tq=tq, tk=tk),
        grid=(b, HEADS, s_len // tq),
        in_specs=[
            pl.BlockSpec((None, tq, HEAD_DIM), lambda bi, h, i: (bi, i, cb + h)),
            pl.BlockSpec((None, s_len, HEAD_DIM), lambda bi, h, i: (bi, 0, cb + HEADS + h)),
            pl.BlockSpec((None, s_len, HEAD_DIM), lambda bi, h, i: (bi, 0, cb + 2 * HEADS + h)),
            pl.BlockSpec((None, s_len, LANES), lambda bi, h, i: (bi, 0, 0)),
        ],
        out_specs=pl.BlockSpec((None, tq, HEAD_DIM), lambda bi, h, i: (bi, i, h)),
        out_shape=jax.ShapeDtypeStruct((b, s_len, HEADS * HEAD_DIM), BF16),
        scratch_shapes=[pltpu.VMEM((s_len, LANES), F32), pltpu.VMEM((HEAD_DIM, tq), BF16),
                        pltpu.VMEM((1, tq), F32), pltpu.VMEM((1, tq), F32), pltpu.VMEM((HEAD_DIM, tq), F32)],
        compiler_params=_cparams(("parallel", "parallel", "arbitrary")),
        name="attn_fox",
    )(proj3, proj3, proj3, cum)


def _attn_c_kernel(q_ref, k_ref, v_ref, o_ref, qt_scr, m_scr, l_scr, acc_scr, *, tq, tk):
    qt_scr[...] = q_ref[...].astype(F32).T.astype(BF16)
    _flash_loop(qt_scr, k_ref, v_ref, m_scr, l_scr, acc_scr, q0=pl.program_id(2) * tq, tq=tq, tk=tk,
                bias_fn=None, causal_chunked=True)
    o_ref[...] = (acc_scr[...] * (1.0 / l_scr[...])).T.astype(o_ref.dtype)


def attn_c_call(q3, k3, v3, tq=1024, tk=512):
    b, s_len, _ = q3.shape
    tq = min(tq, s_len)
    tk = min(tk, tq)
    assert tq % tk == 0 and s_len % tq == 0
    return pl.pallas_call(
        functools.partial(_attn_c_kernel, tq=tq, tk=tk),
        grid=(b, HEADS, s_len // tq),
        in_specs=[
            pl.BlockSpec((None, tq, C_QK_PAD), lambda bi, h, i: (bi, i, h)),
            pl.BlockSpec((None, s_len, C_QK_PAD), lambda bi, h, i: (bi, 0, h)),
            pl.BlockSpec((None, s_len, HEAD_DIM), lambda bi, h, i: (bi, 0, h)),
        ],
        out_specs=pl.BlockSpec((None, tq, HEAD_DIM), lambda bi, h, i: (bi, i, h)),
        out_shape=jax.ShapeDtypeStruct((b, s_len, HEADS * HEAD_DIM), BF16),
        scratch_shapes=[pltpu.VMEM((C_QK_PAD, tq), BF16), pltpu.VMEM((1, tq), F32), pltpu.VMEM((1, tq), F32),
                        pltpu.VMEM((HEAD_DIM, tq), F32)],
        compiler_params=_cparams(("parallel", "parallel", "arbitrary")),
        name="attn_mla",
    )(q3, k3, v3)


def _merge_kernel(x_ref, oa_ref, ob_ref, oc_ref, ga_ref, gb_ref, gc_ref, wb_ref, wo_ref, g_ref,
                  x_out_ref, h_out_ref):
    merged = None
    for gi, (o_ref, gate_ref) in enumerate(((oa_ref, ga_ref), (ob_ref, gb_ref), (oc_ref, gc_ref))):
        br = jnp.dot(o_ref[...], wb_ref[gi], preferred_element_type=F32)
        term = jax.nn.sigmoid(gate_ref[...].astype(F32)) * br
        merged = term if merged is None else merged + term
    x_new = x_ref[...] + jnp.dot(merged.astype(BF16), wo_ref[...], preferred_element_type=F32)
    x_out_ref[...] = x_new
    h_out_ref[...] = _rms(x_new, g_ref[...]).astype(h_out_ref.dtype)


def merge_call(x, o_a, o_b, o_c, proj, w_branch, w_out, norm_g, tm=256):
    m, d = x.shape
    tm = min(tm, m)
    row = lambda width: pl.BlockSpec((tm, width), lambda i: (i, 0))
    gate = lambda gi: pl.BlockSpec((tm, d), lambda i: (i, OFF_GATES // d + gi))
    return pl.pallas_call(
        _merge_kernel,
        grid=(m // tm,),
        in_specs=[
            row(d), row(BRANCH_WIDTH), row(BRANCH_WIDTH), row(BRANCH_WIDTH), gate(0), gate(1), gate(2),
            pl.BlockSpec(w_branch.shape, lambda i: (0, 0, 0), pipeline_mode=pl.Buffered(1)),
            pl.BlockSpec(w_out.shape, lambda i: (0, 0), pipeline_mode=pl.Buffered(1)),
            pl.BlockSpec((1, d), lambda i: (0, 0)),
        ],
        out_specs=[row(d), row(d)],
        out_shape=[jax.ShapeDtypeStruct((m, d), F32), jax.ShapeDtypeStruct((m, d), BF16)],
        compiler_params=_cparams(("parallel",)),
        name="merge_out",
    )(x, o_a, o_b, o_c, proj, proj, proj, w_branch, w_out, norm_g.reshape(1, d))


def _ffn_kernel(h_ref, x_ref, w1_ref, w3_ref, w2_ref, g_ref, *outs, emit_x):
    acc_ref = outs[-1]
    f = pl.program_id(1)

    @pl.when(f == 0)
    def _():
        acc_ref[...] = x_ref[...]

    h = h_ref[...]
    a = jnp.dot(h, w1_ref[...], preferred_element_type=F32)
    b = jnp.dot(h, w3_ref[...], preferred_element_type=F32)
    act = (a * jax.nn.sigmoid(a) * b).astype(BF16)
    acc_ref[...] += jnp.dot(act, w2_ref[...], preferred_element_type=F32)

    @pl.when(f == pl.num_programs(1) - 1)
    def _():
        x_new = acc_ref[...]
        if emit_x:
            outs[0][...] = x_new
        outs[-2][...] = _rms(x_new, g_ref[...]).astype(outs[-2].dtype)


def ffn_call(h, x, w1, w3, w2, norm_g, *, emit_x, norm_dtype, tm=512, tf=512):
    m, d = x.shape
    tm = min(tm, m)
    row = pl.BlockSpec((tm, d), lambda i, f: (i, 0))
    w_in = pl.BlockSpec((d, tf), lambda i, f: (0, f))
    out_specs = [row]
    out_shape = [jax.ShapeDtypeStruct((m, d), norm_dtype)]
    if emit_x:
        out_specs = [row, row]
        out_shape = [jax.ShapeDtypeStruct((m, d), F32)] + out_shape
    return pl.pallas_call(
        functools.partial(_ffn_kernel, emit_x=emit_x),
        grid=(m // tm, w1.shape[1] // tf),
        in_specs=[row, row, w_in, w_in, pl.BlockSpec((tf, d), lambda i, f: (f, 0)),
                  pl.BlockSpec((1, d), lambda i, f: (0, 0))],
        out_specs=out_specs,
        out_shape=out_shape,
        scratch_shapes=[pltpu.VMEM((tm, d), F32)],
        compiler_params=_cparams(("parallel", "arbitrary")),
        name="dense_ffn",
    )(h, x, w1, w3, w2, norm_g.reshape(1, d))


def _router_kernel(x_ref, g_ref, w_ref, i_ref, c_ref):
    h = _rms(x_ref[...], g_ref[...])
    logits = jnp.dot(h, w_ref[...], preferred_element_type=F32, precision=lax.Precision.HIGHEST)
    lane = lax.broadcasted_iota(jnp.int32, logits.shape, 1)
    logits = jnp.where(lane < N_EXPERTS, logits, -jnp.inf)
    m1 = jnp.max(logits, axis=1, keepdims=True)
    i1 = jnp.min(jnp.where(logits == m1, lane, LANES), axis=1, keepdims=True)
    rest = jnp.where(lane == i1, -jnp.inf, logits)
    m2 = jnp.max(rest, axis=1, keepdims=True)
    i2 = jnp.min(jnp.where(rest == m2, lane, LANES), axis=1, keepdims=True)
    e2 = jnp.exp(m2 - m1)
    inv = 1.0 / (1.0 + e2)
    i_ref[...] = jnp.where(lane == 0, i1, jnp.where(lane == 1, i2, 0))
    c_ref[...] = jnp.where(lane == 0, inv, jnp.where(lane == 1, e2 * inv, 0.0))


def router_call(x, norm_g, w_pad, tm=512):
    m, d = x.shape
    tm = min(tm, m)
    out = pl.BlockSpec((tm, LANES), lambda i: (i, 0))
    return pl.pallas_call(
        _router_kernel,
        grid=(m // tm,),
        in_specs=[pl.BlockSpec((tm, d), lambda i: (i, 0)), pl.BlockSpec((1, d), lambda i: (0, 0)),
                  pl.BlockSpec((d, LANES), lambda i: (0, 0))],
        out_specs=[out, out],
        out_shape=[jax.ShapeDtypeStruct((m, LANES), jnp.int32), jax.ShapeDtypeStruct((m, LANES), F32)],
        compiler_params=_cparams(("parallel",)),
        name="router",
    )(x, norm_g.reshape(1, d), w_pad)


def _row_copy(src_hbm, idx_ref, dst_ref, sem, r):
    return pltpu.make_async_copy(src_hbm.at[pl.ds(idx_ref[0, r], 1)], dst_ref.at[pl.ds(r, 1)], sem)


def _gather_rows(src_hbm, idx_ref, dst_ref, sem):
    n = dst_ref.shape[0]

    def start(r, c):
        _row_copy(src_hbm, idx_ref, dst_ref, sem, r).start()
        return c

    def wait(r, c):
        _row_copy(src_hbm, idx_ref, dst_ref, sem, r).wait()
        return c

    lax.fori_loop(0, n, start, 0, unroll=8)
    lax.fori_loop(0, n, wait, 0, unroll=8)


def _gather_kernel(idx_ref, src_hbm, o_ref, sem):
    _gather_rows(src_hbm, idx_ref, o_ref, sem)


def gather_rows_call(src, idx, tg=256):
    n_rows = idx.shape[0]
    width = src.shape[1]
    return pl.pallas_call(
        _gather_kernel,
        grid=(n_rows // tg,),
        in_specs=[pl.BlockSpec((None, 1, tg), lambda i: (i, 0, 0), memory_space=pltpu.SMEM),
                  pl.BlockSpec(memory_space=pl.ANY)],
        out_specs=pl.BlockSpec((tg, width), lambda i: (i, 0)),
        out_shape=jax.ShapeDtypeStruct((n_rows, width), src.dtype),
        scratch_shapes=[pltpu.SemaphoreType.DMA],
        compiler_params=_cparams(("arbitrary",)),
        name="moe_gather",
    )(idx.reshape(n_rows // tg, 1, tg), src)


def _moe_group_kernel(te_ref, nv_ref, xs_ref, g_ref, w1_ref, w3_ref, w2_ref, o_ref, h_scr, acc_scr):
    del te_ref
    f = pl.program_id(1)
    valid = pl.program_id(0) < nv_ref[0]
    last = f == pl.num_programs(1) - 1

    @pl.when(valid & (f == 0))
    def _():
        h_scr[...] = _rms(xs_ref[...], g_ref[...]).astype(h_scr.dtype)
        acc_scr[...] = jnp.zeros(acc_scr.shape, F32)

    @pl.when(valid)
    def _():
        h = h_scr[...]
        a = jnp.dot(h, w1_ref[...], preferred_element_type=F32)
        b = jnp.dot(h, w3_ref[...], preferred_element_type=F32)
        act = (a * jax.nn.sigmoid(a) * b).astype(BF16)
        acc_scr[...] += jnp.dot(act, w2_ref[...], preferred_element_type=F32)

    @pl.when(valid & last)
    def _():
        o_ref[...] = acc_scr[...]

    @pl.when(jnp.logical_not(valid) & last)
    def _():
        o_ref[...] = jnp.zeros(o_ref.shape, o_ref.dtype)


def moe_group_call(xs, norm_g, w1, w3, w2, tile_expert, n_valid, tm, tf=512):
    mp, d = xs.shape
    n_f = w1.shape[-1] // tf

    def f_sel(i, f, nv):
        return jnp.where(i < nv[0], f, n_f - 1)

    w_in = pl.BlockSpec((None, d, tf), lambda i, f, te, nv: (te[i], 0, f_sel(i, f, nv)))
    w_out = pl.BlockSpec((None, tf, d), lambda i, f, te, nv: (te[i], f_sel(i, f, nv), 0))
    row = pl.BlockSpec((tm, d), lambda i, f, te, nv: (i, 0))
    return pl.pallas_call(
        _moe_group_kernel,
        grid_spec=pltpu.PrefetchScalarGridSpec(
            num_scalar_prefetch=2,
            grid=(mp // tm, n_f),
            in_specs=[row, pl.BlockSpec((1, d), lambda i, f, te, nv: (0, 0)), w_in, w_in, w_out],
            out_specs=row,
            scratch_shapes=[pltpu.VMEM((tm, d), BF16), pltpu.VMEM((tm, d), F32)],
        ),
        out_shape=jax.ShapeDtypeStruct((mp, d), F32),
        compiler_params=_cparams(("arbitrary", "arbitrary")),
        name="moe_experts",
    )(tile_expert, n_valid, xs, norm_g.reshape(1, d), w1, w3, w2)


def _moe_combine_kernel(p0_ref, p1_ref, x_ref, c_ref, g_ref, ys_hbm, *rest, emit_x):
    *outs, y0_scr, y1_scr, sem0, sem1 = rest
    n = x_ref.shape[0]

    def start(r, c):
        _row_copy(ys_hbm, p0_ref, y0_scr, sem0, r).start()
        _row_copy(ys_hbm, p1_ref, y1_scr, sem1, r).start()
        return c

    def wait(r, c):
        _row_copy(ys_hbm, p0_ref, y0_scr, sem0, r).wait()
        _row_copy(ys_hbm, p1_ref, y1_scr, sem1, r).wait()
        return c

    lax.fori_loop(0, n, start, 0, unroll=8)
    lax.fori_loop(0, n, wait, 0, unroll=8)
    lane = lax.broadcasted_iota(jnp.int32, c_ref.shape, 1)
    c = c_ref[...]
    c0 = jnp.sum(jnp.where(lane == 0, c, 0.0), axis=1, keepdims=True)
    c1 = jnp.sum(jnp.where(lane == 1, c, 0.0), axis=1, keepdims=True)
    x_new = x_ref[...] + c0 * y0_scr[...] + c1 * y1_scr[...]
    if emit_x:
        outs[0][...] = x_new
    outs[-1][...] = _rms(x_new, g_ref[...]).astype(outs[-1].dtype)


def moe_combine_call(x, weights, ys, pos0, pos1, norm_g, *, emit_x, norm_dtype, tg=256):
    m, d = x.shape
    tg = min(tg, m)
    idx = pl.BlockSpec((None, 1, tg), lambda i: (i, 0, 0), memory_space=pltpu.SMEM)
    row = pl.BlockSpec((tg, d), lambda i: (i, 0))
    out_specs = [row]
    out_shape = [jax.ShapeDtypeStruct((m, d), norm_dtype)]
    if emit_x:
        out_specs = [row, row]
        out_shape = [jax.ShapeDtypeStruct((m, d), F32)] + out_shape
    return pl.pallas_call(
        functools.partial(_moe_combine_kernel, emit_x=emit_x),
        grid=(m // tg,),
        in_specs=[idx, idx, row, pl.BlockSpec((tg, LANES), lambda i: (i, 0)),
                  pl.BlockSpec((1, d), lambda i: (0, 0)), pl.BlockSpec(memory_space=pl.ANY)],
        out_specs=out_specs,
        out_shape=out_shape,
        scratch_shapes=[pltpu.VMEM((tg, d), F32), pltpu.VMEM((tg, d), F32),
                        pltpu.SemaphoreType.DMA, pltpu.SemaphoreType.DMA],
        compiler_params=_cparams(("arbitrary",)),
        name="moe_combine",
    )(pos0.reshape(m // tg, 1, tg), pos1.reshape(m // tg, 1, tg), x, weights, norm_g.reshape(1, d), ys)


def _route(choice, tm):
    t = choice.shape[0]
    n_tiles = (TOP_K * t) // tm + N_EXPERTS
    e = choice.reshape(-1)
    onehot = (e[:, None] == jnp.arange(N_EXPERTS, dtype=jnp.int32)[None, :]).astype(jnp.int32)
    rank = jnp.cumsum(onehot, axis=0) - onehot
    counts = jnp.sum(onehot, axis=0)
    tiles_per = (counts + tm - 1) // tm
    tile_end = jnp.cumsum(tiles_per)
    start_row = (tile_end - tiles_per) * tm
    pos = jnp.sum(onehot * (start_row[None, :] + rank), axis=1)
    src = jnp.zeros((n_tiles * tm,), jnp.int32).at[pos].set(jnp.arange(TOP_K * t, dtype=jnp.int32) // TOP_K)
    n_valid = tile_end[-1:]
    tile_id = jnp.minimum(jnp.arange(n_tiles, dtype=jnp.int32), n_valid[0] - 1)
    tile_expert = jnp.sum((tile_id[:, None] >= tile_end[None, :]).astype(jnp.int32), axis=1)
    return pos.reshape(t, TOP_K), src, tile_expert.astype(jnp.int32), n_valid.astype(jnp.int32)


def _rot_pair(w):
    half = w.shape[-1] // 2
    return jnp.concatenate([-w[..., half:], w[..., :half]], axis=-1)


def _pack_in_proj(w):
    d = w.shape[0]
    bw = BRANCH_WIDTH
    aq, ak, av, bq, bk, bv = (w[:, i * bw:(i + 1) * bw] for i in range(6))
    o = 6 * bw
    bf = w[:, o:o + HEADS]
    o += HEADS
    cq = w[:, o:o + C_RANK]
    ckv = w[:, o + C_RANK:o + 2 * C_RANK]
    o += 2 * C_RANK
    ckr = w[:, o:o + C_ROPE_DIM]
    gates = w[:, o + C_ROPE_DIM:]
    z = lambda n: jnp.zeros((d, n), w.dtype)
    packed = jnp.concatenate(
        [gates, aq * (A_QK_DIM ** -0.5 * LOG2E), ak, av, bq * (HEAD_DIM ** -0.5 * LOG2E), bk, bv, cq, ckv,
         ckr, z(LANES - C_ROPE_DIM), _rot_pair(ckr), z(LANES - C_ROPE_DIM),
         bf, z(LANES - HEADS), z(N_PROJ - OFF_BF - LANES)], axis=1)
    return packed.astype(BF16)


def _pack_mla(w_uq, w_uk, w_uv):
    r = w_uq.shape[0]
    scale = (C_NOPE_DIM + C_ROPE_DIM) ** -0.5 * LOG2E
    nope, rope = w_uq[..., :C_NOPE_DIM] * scale, w_uq[..., C_NOPE_DIM:] * scale
    zpad = jnp.zeros((r, HEADS, C_QK_PAD - C_NOPE_DIM - C_ROPE_DIM), w_uq.dtype)
    wq = jnp.concatenate([nope, rope, zpad], axis=-1).reshape(r, HEADS * C_QK_PAD)
    wqp = jnp.concatenate([_rot_pair(rope), zpad], axis=-1).reshape(r, HEADS * LANES)
    return (wq.astype(BF16), wqp.astype(BF16), w_uk.reshape(r, -1).astype(BF16),
            w_uv.reshape(r, -1).astype(BF16))


def _rope_freqs():
    half = C_ROPE_DIM // 2
    inv = ROPE_THETA ** (-jnp.arange(half, dtype=F32) / half)
    return jnp.concatenate([inv, inv, jnp.zeros((LANES - C_ROPE_DIM,), F32)]).reshape(1, LANES)


def kernel(x, positions, attn_norm_g, w_in, diff_lambda, diff_subln_g, fox_forget_bias, mla_q_norm_g, mla_kv_norm_g, mla_w_uq, mla_w_uk, mla_w_uv, w_branch, w_out, ffn_norm_g, dense_w1, dense_w3, dense_w2, router_w, expert_w1, expert_w3, expert_w2, final_norm_g):
    b, s_len, d = x.shape
    m = b * s_len
    depth = w_in.shape[0]
    posf = positions.astype(F32)
    posq_row = posf.reshape(b, 1, s_len)
    posk_rep = jnp.broadcast_to(posf[:, :, None], (b, s_len, LANES))
    pos_col = posf.reshape(m, 1)
    slopes = jnp.exp2(-8.0 * (jnp.arange(HEADS, dtype=F32) + 1.0) / HEADS)
    freqs = _rope_freqs()

    xr = x.reshape(m, d)
    hn = rmsnorm_call(xr, attn_norm_g[0], BF16)
    out = None
    for l in range(depth):
        lam_init = 0.8 - 0.6 * math.exp(-0.3 * l)
        proj = matmul_call(hn, _pack_in_proj(w_in[l]), BF16)
        proj3 = proj.reshape(b, s_len, N_PROJ)

        o_a = attn_a_call(proj3, posq_row, posk_rep, slopes, diff_lambda[l], diff_subln_g[l], lam_init)
        bias_row = jnp.zeros((1, LANES), F32).at[0, :HEADS].set(fox_forget_bias[l])
        o_b = attn_b_call(proj3, fox_cumsum_call(proj3, bias_row))
        wq, wqp, wk, wv = _pack_mla(mla_w_uq[l], mla_w_uk[l], mla_w_uv[l])
        q_c, k_c, v_c = mla_up_call(proj, pos_col, mla_q_norm_g[l].reshape(1, -1),
                                    mla_kv_norm_g[l].reshape(1, -1), freqs, wq, wqp, wk, wv)
        o_c = attn_c_call(q_c.reshape(b, s_len, -1), k_c.reshape(b, s_len, -1), v_c.reshape(b, s_len, -1))

        x1, h_ffn = merge_call(xr, o_a.reshape(m, -1), o_b.reshape(m, -1), o_c.reshape(m, -1), proj,
                               w_branch[l].astype(BF16), w_out[l].astype(BF16), ffn_norm_g[l])
        last = l == depth - 1
        next_g = final_norm_g if last else attn_norm_g[l + 1]
        j = l // 2
        if l % 2 == 0:
            res = ffn_call(h_ffn, x1, dense_w1[j].astype(BF16), dense_w3[j].astype(BF16),
                           dense_w2[j].astype(BF16), next_g, emit_x=not last,
                           norm_dtype=F32 if last else BF16)
        else:
            w_pad = jnp.zeros((d, LANES), F32).at[:, :N_EXPERTS].set(router_w[j])
            choice, weights = router_call(x1, ffn_norm_g[l], w_pad)
            tm_moe = min(512, m)
            pos, src, tile_expert, n_valid = _route(choice[:, :TOP_K], tm_moe)
            xs = gather_rows_call(x1, src)
            ys = moe_group_call(xs, ffn_norm_g[l], expert_w1[j].astype(BF16), expert_w3[j].astype(BF16),
                                expert_w2[j].astype(BF16), tile_expert, n_valid, tm_moe)
            res = moe_combine_call(x1, weights, ys, pos[:, 0], pos[:, 1], next_g, emit_x=not last,
                                   norm_dtype=F32 if last else BF16)
        if last:
            out = res[0]
        else:
            xr, hn = res
    return out.reshape(b, s_len, d)
```

```python
import functools
import math

import jax
import jax.numpy as jnp
from jax import lax
from jax.experimental import pallas as pl
from jax.experimental.pallas import tpu as pltpu

F32 = jnp.float32
BF16 = jnp.bfloat16

D_MODEL = 2048
CHUNK_SHIFT = 6
NORM_EPS = 1e-6
NEG_INF = -1e30

HEADS = 8
HEAD_DIM = 128
A_QK_DIM = 64
C_RANK = 512
C_NOPE_DIM = 128
C_ROPE_DIM = 64
C_QK_PAD = 256
ROPE_THETA = 10000.0
N_BRANCHES = 3
BRANCH_WIDTH = 1024
N_EXPERTS = 8
TOP_K = 2
LANES = 128

OFF_GATES = 0
OFF_A = 3 * D_MODEL
OFF_B = OFF_A + 3 * BRANCH_WIDTH
OFF_CQ = OFF_B + 3 * BRANCH_WIDTH
OFF_CKV = OFF_CQ + C_RANK
OFF_CKR = OFF_CKV + C_RANK
OFF_BF = OFF_CKR + 2 * LANES
N_PROJ = 13824

VMEM_LIMIT = 56 * 1024 * 1024


def _cparams(sem):
    return pltpu.CompilerParams(dimension_semantics=sem, vmem_limit_bytes=VMEM_LIMIT)


def _rms(x, g):
    return x * lax.rsqrt(jnp.mean(x * x, axis=-1, keepdims=True) + NORM_EPS) * g


def _rmsnorm_kernel(x_ref, g_ref, o_ref):
    o_ref[...] = _rms(x_ref[...].astype(F32), g_ref[...]).astype(o_ref.dtype)


def rmsnorm_call(x, g, out_dtype, tm=512):
    m, d = x.shape
    tm = min(tm, m)
    return pl.pallas_call(
        _rmsnorm_kernel,
        grid=(m // tm,),
        in_specs=[pl.BlockSpec((tm, d), lambda i: (i, 0)), pl.BlockSpec((1, d), lambda i: (0, 0))],
        out_specs=pl.BlockSpec((tm, d), lambda i: (i, 0)),
        out_shape=jax.ShapeDtypeStruct((m, d), out_dtype),
        compiler_params=_cparams(("parallel",)),
        name="rmsnorm",
    )(x, g.reshape(1, d))


def _matmul_kernel(a_ref, b_ref, o_ref):
    o_ref[...] = jnp.dot(a_ref[...], b_ref[...], preferred_element_type=F32).astype(o_ref.dtype)


def matmul_call(a, b, out_dtype, tm=1024, tn=512):
    m, k = a.shape
    _, n = b.shape
    tm = min(tm, m)
    return pl.pallas_call(
        _matmul_kernel,
        grid=(m // tm, n // tn),
        in_specs=[pl.BlockSpec((tm, k), lambda i, j: (i, 0)), pl.BlockSpec((k, tn), lambda i, j: (0, j))],
        out_specs=pl.BlockSpec((tm, tn), lambda i, j: (i, j)),
        out_shape=jax.ShapeDtypeStruct((m, n), out_dtype),
        compiler_params=_cparams(("parallel", "arbitrary")),
        name="in_proj",
    )(a, b)


def _mla_up_kernel(cq_ref, ckv_ref, ckr_ref, pos_ref, gq_ref, gkv_ref, f_ref,
                   wq_ref, wqp_ref, wk_ref, wv_ref, q_ref, k_ref, v_ref):
    cqn = _rms(cq_ref[...].astype(F32), gq_ref[...]).astype(BF16)
    ckvn = _rms(ckv_ref[...].astype(F32), gkv_ref[...]).astype(BF16)
    ang = pos_ref[...] * f_ref[...]
    cos, sin = jnp.cos(ang), jnp.sin(ang)
    q = jnp.dot(cqn, wq_ref[...], preferred_element_type=F32)
    qp = jnp.dot(cqn, wqp_ref[...], preferred_element_type=F32)
    for h in range(HEADS):
        nope = slice(h * C_QK_PAD, h * C_QK_PAD + C_NOPE_DIM)
        rope = slice(h * C_QK_PAD + C_NOPE_DIM, (h + 1) * C_QK_PAD)
        q_ref[:, nope] = q[:, nope].astype(q_ref.dtype)
        q_ref[:, rope] = (q[:, rope] * cos + qp[:, h * LANES:(h + 1) * LANES] * sin).astype(q_ref.dtype)
    ckr = ckr_ref[...].astype(F32)
    kr = (ckr[:, :LANES] * cos + ckr[:, LANES:] * sin).astype(k_ref.dtype)
    kn = jnp.dot(ckvn, wk_ref[...], preferred_element_type=F32).astype(k_ref.dtype)
    for h in range(HEADS):
        k_ref[:, h * C_QK_PAD:h * C_QK_PAD + C_NOPE_DIM] = kn[:, h * C_NOPE_DIM:(h + 1) * C_NOPE_DIM]
        k_ref[:, h * C_QK_PAD + C_NOPE_DIM:(h + 1) * C_QK_PAD] = kr
    v_ref[...] = jnp.dot(ckvn, wv_ref[...], preferred_element_type=F32).astype(v_ref.dtype)


def mla_up_call(proj, posf, gq, gkv, freqs, wq, wqp, wk, wv, tm=512):
    m = proj.shape[0]
    tm = min(tm, m)
    const = lambda shape: pl.BlockSpec(shape, lambda i: (0, 0))
    return pl.pallas_call(
        _mla_up_kernel,
        grid=(m // tm,),
        in_specs=[
            pl.BlockSpec((tm, C_RANK), lambda i: (i, OFF_CQ // C_RANK)),
            pl.BlockSpec((tm, C_RANK), lambda i: (i, OFF_CKV // C_RANK)),
            pl.BlockSpec((tm, 2 * LANES), lambda i: (i, OFF_CKR // (2 * LANES))),
            pl.BlockSpec((tm, 1), lambda i: (i, 0)),
            const((1, C_RANK)), const((1, C_RANK)), const((1, LANES)),
            const(wq.shape), const(wqp.shape), const(wk.shape), const(wv.shape),
        ],
        out_specs=[
            pl.BlockSpec((tm, HEADS * C_QK_PAD), lambda i: (i, 0)),
            pl.BlockSpec((tm, HEADS * C_QK_PAD), lambda i: (i, 0)),
            pl.BlockSpec((tm, HEADS * HEAD_DIM), lambda i: (i, 0)),
        ],
        out_shape=[
            jax.ShapeDtypeStruct((m, HEADS * C_QK_PAD), BF16),
            jax.ShapeDtypeStruct((m, HEADS * C_QK_PAD), BF16),
            jax.ShapeDtypeStruct((m, HEADS * HEAD_DIM), BF16),
        ],
        compiler_params=_cparams(("parallel",)),
        name="mla_up",
    )(proj, proj, proj, posf, gq, gkv, freqs, wq, wqp, wk, wv)


def _fox_cumsum_kernel(f_ref, b_ref, o_ref, *, chunk):
    s_len = f_ref.shape[0]
    r = lax.broadcasted_iota(jnp.int32, (chunk, chunk), 0)
    c = lax.broadcasted_iota(jnp.int32, (chunk, chunk), 1)
    tri = (c <= r).astype(F32)
    carry = jnp.zeros((1, LANES), F32)
    for i in range(s_len // chunk):
        x = f_ref[i * chunk:(i + 1) * chunk, :].astype(F32) + b_ref[...]
        ls = jnp.minimum(x, 0.0) - jnp.log(1.0 + jnp.exp(-jnp.abs(x)))
        cum = jnp.dot(tri, ls, preferred_element_type=F32, precision=lax.Precision.HIGHEST) + carry
        o_ref[i * chunk:(i + 1) * chunk, :] = cum
        carry = cum[chunk - 1:chunk, :]


def fox_cumsum_call(proj3, bias_row):
    b, s_len, _ = proj3.shape
    chunk = min(512, s_len)
    return pl.pallas_call(
        functools.partial(_fox_cumsum_kernel, chunk=chunk),
        grid=(b,),
        in_specs=[pl.BlockSpec((None, s_len, LANES), lambda i: (i, 0, OFF_BF // LANES)),
                  pl.BlockSpec((1, LANES), lambda i: (0, 0))],
        out_specs=pl.BlockSpec((None, s_len, LANES), lambda i: (i, 0, 0)),
        out_shape=jax.ShapeDtypeStruct((b, s_len, LANES), F32),
        compiler_params=_cparams(("parallel",)),
        name="fox_cumsum",
    )(proj3, bias_row)


def _flash_loop(qt_scr, k_ref, v_ref, m_scr, l_scr, acc_scr, gap_scr, *, q0, tq, tk, bias_fn, causal_chunked):
    ncols = qt_scr.shape[1]

    def init():
        m_scr[...] = jnp.full(m_scr.shape, NEG_INF, F32)
        l_scr[...] = jnp.zeros(l_scr.shape, F32)
        acc_scr[...] = jnp.zeros(acc_scr.shape, F32)
        gap_scr[...] = jnp.full(gap_scr.shape, NEG_INF, F32)

    def scores(j, masked, c0):
        w = ncols - c0
        ks = pl.multiple_of(j * tk, tk)
        s = jnp.dot(k_ref[pl.ds(ks, tk), :], qt_scr[:, c0:], preferred_element_type=F32)
        if bias_fn is not None:
            s = bias_fn(s, ks, c0, w)
        if masked:
            kidx = ks + lax.broadcasted_iota(jnp.int32, (tk, w), 0)
            qidx = q0 + ((c0 + lax.broadcasted_iota(jnp.int32, (tk, w), 1)) & (tq - 1))
            if causal_chunked:
                keep = (kidx >> CHUNK_SHIFT) <= (qidx >> CHUNK_SHIFT)
            else:
                keep = kidx <= qidx
            s = jnp.where(keep, s, NEG_INF)
        return s

    def pv_of(p, j):
        ks = pl.multiple_of(j * tk, tk)
        return lax.dot_general(v_ref[pl.ds(ks, tk), :], p.astype(BF16), (((0,), (0,)), ((), ())),
                               preferred_element_type=F32)

    def update_exact(s, j, c0):
        cols = slice(c0, ncols)
        m_prev = m_scr[:, cols]
        m_new = jnp.maximum(m_prev, jnp.max(s, axis=0, keepdims=True))
        alpha = jnp.exp2(m_prev - m_new)
        p = jnp.exp2(s - m_new)
        l_scr[:, cols] = alpha * l_scr[:, cols] + jnp.sum(p, axis=0, keepdims=True)
        acc_scr[:, cols] = alpha * acc_scr[:, cols] + pv_of(p, j)
        m_scr[:, cols] = m_new

    def update_lagged(s, j, c0):
        cols = slice(c0, ncols)
        m_prev = m_scr[:, cols]
        p = jnp.exp2(s - m_prev)
        t_max = jnp.max(s, axis=0, keepdims=True)
        m_new = jnp.maximum(m_prev, t_max)
        beta = jnp.exp2(m_prev - m_new)
        l_scr[:, cols] = (l_scr[:, cols] + jnp.sum(p, axis=0, keepdims=True)) * beta
        acc_scr[:, cols] = (acc_scr[:, cols] + pv_of(p, j)) * beta
        gap_scr[:, cols] = jnp.maximum(gap_scr[:, cols], t_max - m_prev)
        m_scr[:, cols] = m_new

    def steps(tiles, update):
        s_all = [scores(j, masked, c0) for j, masked, c0 in tiles]
        for s, (j, _, c0) in zip(s_all, tiles):
            update(s, j, c0)

    n_full = q0 // tk
    single_map = ncols == tq
    diag = [(n_full + d, True, d * tk if single_map else 0) for d in range(tq // tk)]

    init()
    steps(diag[:1], update_exact)

    def body(i, carry):
        steps([(2 * i, False, 0), (2 * i + 1, False, 0)], update_lagged)
        return carry

    lax.fori_loop(0, n_full // 2, body, 0)
    if (tq // tk) % 2:

        @pl.when(n_full % 2 == 1)
        def _():
            steps([(n_full - 1, False, 0)], update_lagged)

    for i in range(1, len(diag), 2):
        steps(diag[i:i + 2], update_lagged)

    @pl.when(jnp.max(gap_scr[...]) > MAX_GAP)
    def _():
        init()

        def body_exact(j, carry):
            steps([(j, False, 0)], update_exact)
            return carry

        lax.fori_loop(0, n_full, body_exact, 0)
        for tile in diag:
            steps([tile], update_exact)


def _lane_tile(x, ncols):
    return x if ncols == x.shape[1] else jnp.concatenate([x] * (ncols // x.shape[1]), axis=1)


LOG2E = math.log2(math.e)
MAX_GAP = 60.0


def _attn_a_kernel(slopes_ref, q_ref, k_ref, v_ref, posq_ref, posk_ref, lam_ref, g_ref, o_ref,
                   qt_scr, m_scr, l_scr, acc_scr, gap_scr, *, tq, tk, lam_init):
    h = pl.program_id(1)
    q0 = pl.program_id(2) * tq
    q_t = q_ref[...].astype(F32).T
    row = lax.broadcasted_iota(jnp.int32, q_t.shape, 0)
    zero = jnp.zeros_like(q_t)
    qt_scr[:, :tq] = jnp.where(row < A_QK_DIM, q_t, zero).astype(BF16)
    qt_scr[:, tq:] = jnp.where(row >= A_QK_DIM, q_t, zero).astype(BF16)
    slope = slopes_ref[h] * LOG2E
    pq = _lane_tile(posq_ref[...], 2 * tq)

    def bias_fn(s, ks, c0, w):
        pk = _lane_tile(posk_ref[pl.ds(ks, tk), :], w)
        return s - slope * jnp.abs(pq[:, c0:c0 + w] - pk)

    _flash_loop(qt_scr, k_ref, v_ref, m_scr, l_scr, acc_scr, gap_scr,q0=q0, tq=tq, tk=tk, bias_fn=bias_fn,
                causal_chunked=True)
    dl = lam_ref[...]
    lam = (jnp.exp(jnp.sum(dl[0:1] * dl[1:2], axis=1, keepdims=True))
           - jnp.exp(jnp.sum(dl[2:3] * dl[3:4], axis=1, keepdims=True)) + lam_init)
    accn = acc_scr[...] * (1.0 / l_scr[...])
    o = (accn[:, :tq] - lam * accn[:, tq:]).T
    o_ref[...] = (_rms(o, g_ref[...]) * (1.0 - lam_init)).astype(o_ref.dtype)


def attn_a_call(proj3, posq_row, posk_rep, slopes, diff_lambda, subln_g, lam_init, tq=512, tk=512):
    b, s_len, _ = proj3.shape
    tq = min(tq, s_len)
    tk = min(tk, tq)
    assert tq % tk == 0 and s_len % tq == 0
    kern = functools.partial(_attn_a_kernel, tq=tq, tk=tk, lam_init=lam_init)
    cb = OFF_A // HEAD_DIM
    return pl.pallas_call(
        kern,
        grid=(b, HEADS, s_len // tq),
        in_specs=[
            pl.BlockSpec(memory_space=pltpu.SMEM),
            pl.BlockSpec((None, tq, HEAD_DIM), lambda bi, h, i: (bi, i, cb + h)),
            pl.BlockSpec((None, s_len, HEAD_DIM), lambda bi, h, i: (bi, 0, cb + HEADS + h)),
            pl.BlockSpec((None, s_len, HEAD_DIM), lambda bi, h, i: (bi, 0, cb + 2 * HEADS + h)),
            pl.BlockSpec((None, 1, tq), lambda bi, h, i: (bi, 0, i)),
            pl.BlockSpec((None, s_len, LANES), lambda bi, h, i: (bi, 0, 0)),
            pl.BlockSpec((4, A_QK_DIM), lambda bi, h, i: (0, 0)),
            pl.BlockSpec((1, HEAD_DIM), lambda bi, h, i: (0, 0)),
        ],
        out_specs=pl.BlockSpec((None, tq, HEAD_DIM), lambda bi, h, i: (bi, i, h)),
        out_shape=jax.ShapeDtypeStruct((b, s_len, HEADS * HEAD_DIM), BF16),
        scratch_shapes=[pltpu.VMEM((HEAD_DIM, 2 * tq), BF16), pltpu.VMEM((1, 2 * tq), F32),
                        pltpu.VMEM((1, 2 * tq), F32), pltpu.VMEM((HEAD_DIM, 2 * tq), F32),
                        pltpu.VMEM((1, 2 * tq), F32)],
        compiler_params=_cparams(("parallel", "parallel", "arbitrary")),
        name="attn_diff",
    )(slopes, proj3, proj3, proj3, posq_row, posk_rep, diff_lambda, subln_g.reshape(1, HEAD_DIM))


def _attn_b_kernel(q_ref, k_ref, v_ref, cum_ref, o_ref, col_scr, qt_scr, m_scr, l_scr, acc_scr, gap_scr, *, tq, tk):
    h = pl.program_id(1)
    qi = pl.program_id(2)

    @pl.when(qi == 0)
    def _():
        lane = lax.broadcasted_iota(jnp.int32, cum_ref.shape, 1)
        col = jnp.sum(jnp.where(lane == h, cum_ref[...], 0.0), axis=1, keepdims=True)
        col_scr[...] = jnp.broadcast_to(col * LOG2E, col_scr.shape)

    qt_scr[...] = q_ref[...].astype(F32).T.astype(BF16)

    def bias_fn(s, ks, c0, w):
        return s - _lane_tile(col_scr[pl.ds(ks, tk), :], w)

    _flash_loop(qt_scr, k_ref, v_ref, m_scr, l_scr, acc_scr, gap_scr,q0=qi * tq, tq=tq, tk=tk, bias_fn=bias_fn,
                causal_chunked=False)
    o_ref[...] = (acc_scr[...] * (1.0 / l_scr[...])).T.astype(o_ref.dtype)


def attn_b_call(proj3, cum, tq=1024, tk=512):
    b, s_len, _ = proj3.shape
    tq = min(tq, s_len)
    tk = min(tk, tq)
    assert tq % tk == 0 and s_len % tq == 0
    cb = OFF_B // HEAD_DIM
    return pl.pallas_call(
        functools.partial(_attn_b_kernel, tq=tq, tk=tk),
        grid=(b, HEADS, s_len // tq),
        in_specs=[
            pl.BlockSpec((None, tq, HEAD_DIM), lambda bi, h, i: (bi, i, cb + h)),
            pl.BlockSpec((None, s_len, HEAD_DIM), lambda bi, h, i: (bi, 0, cb + HEADS + h)),
            pl.BlockSpec((None, s_len, HEAD_DIM), lambda bi, h, i: (bi, 0, cb + 2 * HEADS + h)),
            pl.BlockSpec((None, s_len, LANES), lambda bi, h, i: (bi, 0, 0)),
        ],
        out_specs=pl.BlockSpec((None, tq, HEAD_DIM), lambda bi, h, i: (bi, i, h)),
        out_shape=jax.ShapeDtypeStruct((b, s_len, HEADS * HEAD_DIM), BF16),
        scratch_shapes=[pltpu.VMEM((s_len, LANES), F32), pltpu.VMEM((HEAD_DIM, tq), BF16),
                        pltpu.VMEM((1, tq), F32), pltpu.VMEM((1, tq), F32), pltpu.VMEM((HEAD_DIM, tq), F32),
                        pltpu.VMEM((1, tq), F32)],
        compiler_params=_cparams(("parallel", "parallel", "arbitrary")),
        name="attn_fox",
    )(proj3, proj3, proj3, cum)


def _attn_c_kernel(q_ref, k_ref, v_ref, o_ref, qt_scr, m_scr, l_scr, acc_scr, gap_scr, *, tq, tk):
    qt_scr[...] = q_ref[...].astype(F32).T.astype(BF16)
    _flash_loop(qt_scr, k_ref, v_ref, m_scr, l_scr, acc_scr, gap_scr,q0=pl.program_id(2) * tq, tq=tq, tk=tk,
                bias_fn=None, causal_chunked=True)
    o_ref[...] = (acc_scr[...] * (1.0 / l_scr[...])).T.astype(o_ref.dtype)


def attn_c_call(q3, k3, v3, tq=1024, tk=512):
    b, s_len, _ = q3.shape
    tq = min(tq, s_len)
    tk = min(tk, tq)
    assert tq % tk == 0 and s_len % tq == 0
    return pl.pallas_call(
        functools.partial(_attn_c_kernel, tq=tq, tk=tk),
        grid=(b, HEADS, s_len // tq),
        in_specs=[
            pl.BlockSpec((None, tq, C_QK_PAD), lambda bi, h, i: (bi, i, h)),
            pl.BlockSpec((None, s_len, C_QK_PAD), lambda bi, h, i: (bi, 0, h)),
            pl.BlockSpec((None, s_len, HEAD_DIM), lambda bi, h, i: (bi, 0, h)),
        ],
        out_specs=pl.BlockSpec((None, tq, HEAD_DIM), lambda bi, h, i: (bi, i, h)),
        out_shape=jax.ShapeDtypeStruct((b, s_len, HEADS * HEAD_DIM), BF16),
        scratch_shapes=[pltpu.VMEM((C_QK_PAD, tq), BF16), pltpu.VMEM((1, tq), F32), pltpu.VMEM((1, tq), F32),
                        pltpu.VMEM((HEAD_DIM, tq), F32), pltpu.VMEM((1, tq), F32)],
        compiler_params=_cparams(("parallel", "parallel", "arbitrary")),
        name="attn_mla",
    )(q3, k3, v3)


def _merge_kernel(x_ref, oa_ref, ob_ref, oc_ref, ga_ref, gb_ref, gc_ref, wb_ref, wo_ref, g_ref,
                  x_out_ref, h_out_ref):
    merged = None
    for gi, (o_ref, gate_ref) in enumerate(((oa_ref, ga_ref), (ob_ref, gb_ref), (oc_ref, gc_ref))):
        br = jnp.dot(o_ref[...], wb_ref[gi], preferred_element_type=F32)
        term = jax.nn.sigmoid(gate_ref[...].astype(F32)) * br
        merged = term if merged is None else merged + term
    x_new = x_ref[...] + jnp.dot(merged.astype(BF16), wo_ref[...], preferred_element_type=F32)
    x_out_ref[...] = x_new
    h_out_ref[...] = _rms(x_new, g_ref[...]).astype(h_out_ref.dtype)


def merge_call(x, o_a, o_b, o_c, proj, w_branch, w_out, norm_g, tm=256):
    m, d = x.shape
    tm = min(tm, m)
    row = lambda width: pl.BlockSpec((tm, width), lambda i: (i, 0))
    gate = lambda gi: pl.BlockSpec((tm, d), lambda i: (i, OFF_GATES // d + gi))
    return pl.pallas_call(
        _merge_kernel,
        grid=(m // tm,),
        in_specs=[
            row(d), row(BRANCH_WIDTH), row(BRANCH_WIDTH), row(BRANCH_WIDTH), gate(0), gate(1), gate(2),
            pl.BlockSpec(w_branch.shape, lambda i: (0, 0, 0), pipeline_mode=pl.Buffered(1)),
            pl.BlockSpec(w_out.shape, lambda i: (0, 0), pipeline_mode=pl.Buffered(1)),
            pl.BlockSpec((1, d), lambda i: (0, 0)),
        ],
        out_specs=[row(d), row(d)],
        out_shape=[jax.ShapeDtypeStruct((m, d), F32), jax.ShapeDtypeStruct((m, d), BF16)],
        compiler_params=_cparams(("parallel",)),
        name="merge_out",
    )(x, o_a, o_b, o_c, proj, proj, proj, w_branch, w_out, norm_g.reshape(1, d))


def _ffn_kernel(h_ref, x_ref, w1_ref, w3_ref, w2_ref, g_ref, *outs, emit_x):
    acc_ref = outs[-1]
    f = pl.program_id(1)

    @pl.when(f == 0)
    def _():
        acc_ref[...] = x_ref[...]

    h = h_ref[...]
    a = jnp.dot(h, w1_ref[...], preferred_element_type=F32)
    b = jnp.dot(h, w3_ref[...], preferred_element_type=F32)
    act = (a * jax.nn.sigmoid(a) * b).astype(BF16)
    acc_ref[...] += jnp.dot(act, w2_ref[...], preferred_element_type=F32)

    @pl.when(f == pl.num_programs(1) - 1)
    def _():
        x_new = acc_ref[...]
        if emit_x:
            outs[0][...] = x_new
        outs[-2][...] = _rms(x_new, g_ref[...]).astype(outs[-2].dtype)


def ffn_call(h, x, w1, w3, w2, norm_g, *, emit_x, norm_dtype, tm=512, tf=512):
    m, d = x.shape
    tm = min(tm, m)
    row = pl.BlockSpec((tm, d), lambda i, f: (i, 0))
    w_in = pl.BlockSpec((d, tf), lambda i, f: (0, f))
    out_specs = [row]
    out_shape = [jax.ShapeDtypeStruct((m, d), norm_dtype)]
    if emit_x:
        out_specs = [row, row]
        out_shape = [jax.ShapeDtypeStruct((m, d), F32)] + out_shape
    return pl.pallas_call(
        functools.partial(_ffn_kernel, emit_x=emit_x),
        grid=(m // tm, w1.shape[1] // tf),
        in_specs=[row, row, w_in, w_in, pl.BlockSpec((tf, d), lambda i, f: (f, 0)),
                  pl.BlockSpec((1, d), lambda i, f: (0, 0))],
        out_specs=out_specs,
        out_shape=out_shape,
        scratch_shapes=[pltpu.VMEM((tm, d), F32)],
        compiler_params=_cparams(("parallel", "arbitrary")),
        name="dense_ffn",
    )(h, x, w1, w3, w2, norm_g.reshape(1, d))


def _router_kernel(x_ref, g_ref, w_ref, i_ref, c_ref):
    h = _rms(x_ref[...], g_ref[...])
    logits = jnp.dot(h, w_ref[...], preferred_element_type=F32, precision=lax.Precision.HIGHEST)
    lane = lax.broadcasted_iota(jnp.int32, logits.shape, 1)
    logits = jnp.where(lane < N_EXPERTS, logits, -jnp.inf)
    m1 = jnp.max(logits, axis=1, keepdims=True)
    i1 = jnp.min(jnp.where(logits == m1, lane, LANES), axis=1, keepdims=True)
    rest = jnp.where(lane == i1, -jnp.inf, logits)
    m2 = jnp.max(rest, axis=1, keepdims=True)
    i2 = jnp.min(jnp.where(rest == m2, lane, LANES), axis=1, keepdims=True)
    e2 = jnp.exp(m2 - m1)
    inv = 1.0 / (1.0 + e2)
    i_ref[...] = jnp.where(lane == 0, i1, jnp.where(lane == 1, i2, 0))
    c_ref[...] = jnp.where(lane == 0, inv, jnp.where(lane == 1, e2 * inv, 0.0))


def router_call(x, norm_g, w_pad, tm=512):
    m, d = x.shape
    tm = min(tm, m)
    out = pl.BlockSpec((tm, LANES), lambda i: (i, 0))
    return pl.pallas_call(
        _router_kernel,
        grid=(m // tm,),
        in_specs=[pl.BlockSpec((tm, d), lambda i: (i, 0)), pl.BlockSpec((1, d), lambda i: (0, 0)),
                  pl.BlockSpec((d, LANES), lambda i: (0, 0))],
        out_specs=[out, out],
        out_shape=[jax.ShapeDtypeStruct((m, LANES), jnp.int32), jax.ShapeDtypeStruct((m, LANES), F32)],
        compiler_params=_cparams(("parallel",)),
        name="router",
    )(x, norm_g.reshape(1, d), w_pad)


def _row_copy(src_hbm, idx_ref, dst_ref, sem, r):
    return pltpu.make_async_copy(src_hbm.at[pl.ds(idx_ref[0, r], 1)], dst_ref.at[pl.ds(r, 1)], sem)


def _gather_rows(src_hbm, idx_ref, dst_ref, sem):
    n = dst_ref.shape[0]

    def start(r, c):
        _row_copy(src_hbm, idx_ref, dst_ref, sem, r).start()
        return c

    def wait(r, c):
        _row_copy(src_hbm, idx_ref, dst_ref, sem, r).wait()
        return c

    lax.fori_loop(0, n, start, 0, unroll=8)
    lax.fori_loop(0, n, wait, 0, unroll=8)


def _gather_kernel(idx_ref, src_hbm, o_ref, sem):
    _gather_rows(src_hbm, idx_ref, o_ref, sem)


def gather_rows_call(src, idx, tg=256):
    n_rows = idx.shape[0]
    width = src.shape[1]
    return pl.pallas_call(
        _gather_kernel,
        grid=(n_rows // tg,),
        in_specs=[pl.BlockSpec((None, 1, tg), lambda i: (i, 0, 0), memory_space=pltpu.SMEM),
                  pl.BlockSpec(memory_space=pl.ANY)],
        out_specs=pl.BlockSpec((tg, width), lambda i: (i, 0)),
        out_shape=jax.ShapeDtypeStruct((n_rows, width), src.dtype),
        scratch_shapes=[pltpu.SemaphoreType.DMA],
        compiler_params=_cparams(("arbitrary",)),
        name="moe_gather",
    )(idx.reshape(n_rows // tg, 1, tg), src)


def _moe_group_kernel(te_ref, nv_ref, xs_ref, g_ref, w1_ref, w3_ref, w2_ref, o_ref, h_scr, acc_scr):
    del te_ref
    f = pl.program_id(1)
    valid = pl.program_id(0) < nv_ref[0]
    last = f == pl.num_programs(1) - 1

    @pl.when(valid & (f == 0))
    def _():
        h_scr[...] = _rms(xs_ref[...], g_ref[...]).astype(h_scr.dtype)
        acc_scr[...] = jnp.zeros(acc_scr.shape, F32)

    @pl.when(valid)
    def _():
        h = h_scr[...]
        a = jnp.dot(h, w1_ref[...], preferred_element_type=F32)
        b = jnp.dot(h, w3_ref[...], preferred_element_type=F32)
        act = (a * jax.nn.sigmoid(a) * b).astype(BF16)
        acc_scr[...] += jnp.dot(act, w2_ref[...], preferred_element_type=F32)

    @pl.when(valid & last)
    def _():
        o_ref[...] = acc_scr[...]

    @pl.when(jnp.logical_not(valid) & last)
    def _():
        o_ref[...] = jnp.zeros(o_ref.shape, o_ref.dtype)


def moe_group_call(xs, norm_g, w1, w3, w2, tile_expert, n_valid, tm, tf=512):
    mp, d = xs.shape
    n_f = w1.shape[-1] // tf

    def f_sel(i, f, nv):
        return jnp.where(i < nv[0], f, n_f - 1)

    w_in = pl.BlockSpec((None, d, tf), lambda i, f, te, nv: (te[i], 0, f_sel(i, f, nv)))
    w_out = pl.BlockSpec((None, tf, d), lambda i, f, te, nv: (te[i], f_sel(i, f, nv), 0))
    row = pl.BlockSpec((tm, d), lambda i, f, te, nv: (i, 0))
    return pl.pallas_call(
        _moe_group_kernel,
        grid_spec=pltpu.PrefetchScalarGridSpec(
            num_scalar_prefetch=2,
            grid=(mp // tm, n_f),
            in_specs=[row, pl.BlockSpec((1, d), lambda i, f, te, nv: (0, 0)), w_in, w_in, w_out],
            out_specs=row,
            scratch_shapes=[pltpu.VMEM((tm, d), BF16), pltpu.VMEM((tm, d), F32)],
        ),
        out_shape=jax.ShapeDtypeStruct((mp, d), F32),
        compiler_params=_cparams(("arbitrary", "arbitrary")),
        name="moe_experts",
    )(tile_expert, n_valid, xs, norm_g.reshape(1, d), w1, w3, w2)


def _moe_combine_kernel(p0_ref, p1_ref, x_ref, c_ref, g_ref, ys_hbm, *rest, emit_x):
    *outs, y0_scr, y1_scr, sem0, sem1 = rest
    n = x_ref.shape[0]

    def start(r, c):
        _row_copy(ys_hbm, p0_ref, y0_scr, sem0, r).start()
        _row_copy(ys_hbm, p1_ref, y1_scr, sem1, r).start()
        return c

    def wait(r, c):
        _row_copy(ys_hbm, p0_ref, y0_scr, sem0, r).wait()
        _row_copy(ys_hbm, p1_ref, y1_scr, sem1, r).wait()
        return c

    lax.fori_loop(0, n, start, 0, unroll=8)
    lax.fori_loop(0, n, wait, 0, unroll=8)
    lane = lax.broadcasted_iota(jnp.int32, c_ref.shape, 1)
    c = c_ref[...]
    c0 = jnp.sum(jnp.where(lane == 0, c, 0.0), axis=1, keepdims=True)
    c1 = jnp.sum(jnp.where(lane == 1, c, 0.0), axis=1, keepdims=True)
    x_new = x_ref[...] + c0 * y0_scr[...] + c1 * y1_scr[...]
    if emit_x:
        outs[0][...] = x_new
    outs[-1][...] = _rms(x_new, g_ref[...]).astype(outs[-1].dtype)


def moe_combine_call(x, weights, ys, pos0, pos1, norm_g, *, emit_x, norm_dtype, tg=256):
    m, d = x.shape
    tg = min(tg, m)
    idx = pl.BlockSpec((None, 1, tg), lambda i: (i, 0, 0), memory_space=pltpu.SMEM)
    row = pl.BlockSpec((tg, d), lambda i: (i, 0))
    out_specs = [row]
    out_shape = [jax.ShapeDtypeStruct((m, d), norm_dtype)]
    if emit_x:
        out_specs = [row, row]
        out_shape = [jax.ShapeDtypeStruct((m, d), F32)] + out_shape
    return pl.pallas_call(
        functools.partial(_moe_combine_kernel, emit_x=emit_x),
        grid=(m // tg,),
        in_specs=[idx, idx, row, pl.BlockSpec((tg, LANES), lambda i: (i, 0)),
                  pl.BlockSpec((1, d), lambda i: (0, 0)), pl.BlockSpec(memory_space=pl.ANY)],
        out_specs=out_specs,
        out_shape=out_shape,
        scratch_shapes=[pltpu.VMEM((tg, d), F32), pltpu.VMEM((tg, d), F32),
                        pltpu.SemaphoreType.DMA, pltpu.SemaphoreType.DMA],
        compiler_params=_cparams(("arbitrary",)),
        name="moe_combine",
    )(pos0.reshape(m // tg, 1, tg), pos1.reshape(m // tg, 1, tg), x, weights, norm_g.reshape(1, d), ys)


def _route(choice, tm):
    t = choice.shape[0]
    n_tiles = (TOP_K * t) // tm + N_EXPERTS
    e = choice.reshape(-1)
    onehot = (e[:, None] == jnp.arange(N_EXPERTS, dtype=jnp.int32)[None, :]).astype(jnp.int32)
    rank = jnp.cumsum(onehot, axis=0) - onehot
    counts = jnp.sum(onehot, axis=0)
    tiles_per = (counts + tm - 1) // tm
    tile_end = jnp.cumsum(tiles_per)
    start_row = (tile_end - tiles_per) * tm
    pos = jnp.sum(onehot * (start_row[None, :] + rank), axis=1)
    src = jnp.zeros((n_tiles * tm,), jnp.int32).at[pos].set(jnp.arange(TOP_K * t, dtype=jnp.int32) // TOP_K)
    n_valid = tile_end[-1:]
    tile_id = jnp.minimum(jnp.arange(n_tiles, dtype=jnp.int32), n_valid[0] - 1)
    tile_expert = jnp.sum((tile_id[:, None] >= tile_end[None, :]).astype(jnp.int32), axis=1)
    return pos.reshape(t, TOP_K), src, tile_expert.astype(jnp.int32), n_valid.astype(jnp.int32)


def _rot_pair(w):
    half = w.shape[-1] // 2
    return jnp.concatenate([-w[..., half:], w[..., :half]], axis=-1)


def _pack_in_proj(w):
    d = w.shape[0]
    bw = BRANCH_WIDTH
    aq, ak, av, bq, bk, bv = (w[:, i * bw:(i + 1) * bw] for i in range(6))
    o = 6 * bw
    bf = w[:, o:o + HEADS]
    o += HEADS
    cq = w[:, o:o + C_RANK]
    ckv = w[:, o + C_RANK:o + 2 * C_RANK]
    o += 2 * C_RANK
    ckr = w[:, o:o + C_ROPE_DIM]
    gates = w[:, o + C_ROPE_DIM:]
    z = lambda n: jnp.zeros((d, n), w.dtype)
    packed = jnp.concatenate(
        [gates, aq * (A_QK_DIM ** -0.5 * LOG2E), ak, av, bq * (HEAD_DIM ** -0.5 * LOG2E), bk, bv, cq, ckv,
         ckr, z(LANES - C_ROPE_DIM), _rot_pair(ckr), z(LANES - C_ROPE_DIM),
         bf, z(LANES - HEADS), z(N_PROJ - OFF_BF - LANES)], axis=1)
    return packed.astype(BF16)


def _pack_mla(w_uq, w_uk, w_uv):
    r = w_uq.shape[0]
    scale = (C_NOPE_DIM + C_ROPE_DIM) ** -0.5 * LOG2E
    nope, rope = w_uq[..., :C_NOPE_DIM] * scale, w_uq[..., C_NOPE_DIM:] * scale
    zpad = jnp.zeros((r, HEADS, C_QK_PAD - C_NOPE_DIM - C_ROPE_DIM), w_uq.dtype)
    wq = jnp.concatenate([nope, rope, zpad], axis=-1).reshape(r, HEADS * C_QK_PAD)
    wqp = jnp.concatenate([_rot_pair(rope), zpad], axis=-1).reshape(r, HEADS * LANES)
    return (wq.astype(BF16), wqp.astype(BF16), w_uk.reshape(r, -1).astype(BF16),
            w_uv.reshape(r, -1).astype(BF16))


def _rope_freqs():
    half = C_ROPE_DIM // 2
    inv = ROPE_THETA ** (-jnp.arange(half, dtype=F32) / half)
    return jnp.concatenate([inv, inv, jnp.zeros((LANES - C_ROPE_DIM,), F32)]).reshape(1, LANES)


def kernel(x, positions, attn_norm_g, w_in, diff_lambda, diff_subln_g, fox_forget_bias, mla_q_norm_g, mla_kv_norm_g, mla_w_uq, mla_w_uk, mla_w_uv, w_branch, w_out, ffn_norm_g, dense_w1, dense_w3, dense_w2, router_w, expert_w1, expert_w3, expert_w2, final_norm_g):
    b, s_len, d = x.shape
    m = b * s_len
    depth = w_in.shape[0]
    posf = positions.astype(F32)
    posq_row = posf.reshape(b, 1, s_len)
    posk_rep = jnp.broadcast_to(posf[:, :, None], (b, s_len, LANES))
    pos_col = posf.reshape(m, 1)
    slopes = jnp.exp2(-8.0 * (jnp.arange(HEADS, dtype=F32) + 1.0) / HEADS)
    freqs = _rope_freqs()

    xr = x.reshape(m, d)
    hn = rmsnorm_call(xr, attn_norm_g[0], BF16)
    out = None
    for l in range(depth):
        lam_init = 0.8 - 0.6 * math.exp(-0.3 * l)
        proj = matmul_call(hn, _pack_in_proj(w_in[l]), BF16)
        proj3 = proj.reshape(b, s_len, N_PROJ)

        o_a = attn_a_call(proj3, posq_row, posk_rep, slopes, diff_lambda[l], diff_subln_g[l], lam_init)
        bias_row = jnp.zeros((1, LANES), F32).at[0, :HEADS].set(fox_forget_bias[l])
        o_b = attn_b_call(proj3, fox_cumsum_call(proj3, bias_row))
        wq, wqp, wk, wv = _pack_mla(mla_w_uq[l], mla_w_uk[l], mla_w_uv[l])
        q_c, k_c, v_c = mla_up_call(proj, pos_col, mla_q_norm_g[l].reshape(1, -1),
                                    mla_kv_norm_g[l].reshape(1, -1), freqs, wq, wqp, wk, wv)
        o_c = attn_c_call(q_c.reshape(b, s_len, -1), k_c.reshape(b, s_len, -1), v_c.reshape(b, s_len, -1))

        x1, h_ffn = merge_call(xr, o_a.reshape(m, -1), o_b.reshape(m, -1), o_c.reshape(m, -1), proj,
                               w_branch[l].astype(BF16), w_out[l].astype(BF16), ffn_norm_g[l])
        last = l == depth - 1
        next_g = final_norm_g if last else attn_norm_g[l + 1]
        j = l // 2
        if l % 2 == 0:
            res = ffn_call(h_ffn, x1, dense_w1[j].astype(BF16), dense_w3[j].astype(BF16),
                           dense_w2[j].astype(BF16), next_g, emit_x=not last,
                           norm_dtype=F32 if last else BF16)
        else:
            w_pad = jnp.zeros((d, LANES), F32).at[:, :N_EXPERTS].set(router_w[j])
            choice, weights = router_call(x1, ffn_norm_g[l], w_pad)
            tm_moe = min(512, m)
            pos, src, tile_expert, n_valid = _route(choice[:, :TOP_K], tm_moe)
            xs = gather_rows_call(x1, src)
            ys = moe_group_call(xs, ffn_norm_g[l], expert_w1[j].astype(BF16), expert_w3[j].astype(BF16),
                                expert_w2[j].astype(BF16), tile_expert, n_valid, tm_moe)
            res = moe_combine_call(x1, weights, ys, pos[:, 0], pos[:, 1], next_g, emit_x=not last,
                                   norm_dtype=F32 if last else BF16)
        if last:
            out = res[0]
        else:
            xr, hn = res
    return out.reshape(b, s_len, d)
```

```python
import functools
import math

import jax
import jax.numpy as jnp
from jax import lax
from jax.experimental import pallas as pl
from jax.experimental.pallas import tpu as pltpu

F32 = jnp.float32
BF16 = jnp.bfloat16

D_MODEL = 2048
CHUNK_SHIFT = 6
NORM_EPS = 1e-6
NEG_INF = -1e30

HEADS = 8
HEAD_DIM = 128
A_QK_DIM = 64
C_RANK = 512
C_NOPE_DIM = 128
C_ROPE_DIM = 64
C_QK_PAD = 256
ROPE_THETA = 10000.0
N_BRANCHES = 3
BRANCH_WIDTH = 1024
N_EXPERTS = 8
TOP_K = 2
LANES = 128

OFF_GATES = 0
OFF_A = 3 * D_MODEL
OFF_B = OFF_A + 3 * BRANCH_WIDTH
OFF_CQ = OFF_B + 3 * BRANCH_WIDTH
OFF_CKV = OFF_CQ + C_RANK
OFF_CKR = OFF_CKV + C_RANK
OFF_BF = OFF_CKR + 2 * LANES
N_PROJ = 13824

VMEM_LIMIT = 56 * 1024 * 1024


def _cparams(sem):
    return pltpu.CompilerParams(dimension_semantics=sem, vmem_limit_bytes=VMEM_LIMIT)


def _rms(x, g):
    return x * lax.rsqrt(jnp.mean(x * x, axis=-1, keepdims=True) + NORM_EPS) * g


def _rmsnorm_kernel(x_ref, g_ref, o_ref):
    o_ref[...] = _rms(x_ref[...].astype(F32), g_ref[...]).astype(o_ref.dtype)


def rmsnorm_call(x, g, out_dtype, tm=512):
    m, d = x.shape
    tm = min(tm, m)
    return pl.pallas_call(
        _rmsnorm_kernel,
        grid=(m // tm,),
        in_specs=[pl.BlockSpec((tm, d), lambda i: (i, 0)), pl.BlockSpec((1, d), lambda i: (0, 0))],
        out_specs=pl.BlockSpec((tm, d), lambda i: (i, 0)),
        out_shape=jax.ShapeDtypeStruct((m, d), out_dtype),
        compiler_params=_cparams(("parallel",)),
        name="rmsnorm",
    )(x, g.reshape(1, d))


def _matmul_kernel(a_ref, b_ref, o_ref):
    o_ref[...] = jnp.dot(a_ref[...], b_ref[...], preferred_element_type=F32).astype(o_ref.dtype)


def matmul_call(a, b, out_dtype, tm=1024, tn=512):
    m, k = a.shape
    _, n = b.shape
    tm = min(tm, m)
    return pl.pallas_call(
        _matmul_kernel,
        grid=(m // tm, n // tn),
        in_specs=[pl.BlockSpec((tm, k), lambda i, j: (i, 0)), pl.BlockSpec((k, tn), lambda i, j: (0, j))],
        out_specs=pl.BlockSpec((tm, tn), lambda i, j: (i, j)),
        out_shape=jax.ShapeDtypeStruct((m, n), out_dtype),
        compiler_params=_cparams(("parallel", "arbitrary")),
        name="in_proj",
    )(a, b)


def _mla_up_kernel(cq_ref, ckv_ref, ckr_ref, pos_ref, gq_ref, gkv_ref, f_ref,
                   wq_ref, wqp_ref, wk_ref, wv_ref, q_ref, k_ref, v_ref):
    cqn = _rms(cq_ref[...].astype(F32), gq_ref[...]).astype(BF16)
    ckvn = _rms(ckv_ref[...].astype(F32), gkv_ref[...]).astype(BF16)
    ang = pos_ref[...] * f_ref[...]
    cos, sin = jnp.cos(ang), jnp.sin(ang)
    q = jnp.dot(cqn, wq_ref[...], preferred_element_type=F32)
    qp = jnp.dot(cqn, wqp_ref[...], preferred_element_type=F32)
    for h in range(HEADS):
        nope = slice(h * C_QK_PAD, h * C_QK_PAD + C_NOPE_DIM)
        rope = slice(h * C_QK_PAD + C_NOPE_DIM, (h + 1) * C_QK_PAD)
        q_ref[:, nope] = q[:, nope].astype(q_ref.dtype)
        q_ref[:, rope] = (q[:, rope] * cos + qp[:, h * LANES:(h + 1) * LANES] * sin).astype(q_ref.dtype)
    ckr = ckr_ref[...].astype(F32)
    kr = (ckr[:, :LANES] * cos + ckr[:, LANES:] * sin).astype(k_ref.dtype)
    kn = jnp.dot(ckvn, wk_ref[...], preferred_element_type=F32).astype(k_ref.dtype)
    for h in range(HEADS):
        k_ref[:, h * C_QK_PAD:h * C_QK_PAD + C_NOPE_DIM] = kn[:, h * C_NOPE_DIM:(h + 1) * C_NOPE_DIM]
        k_ref[:, h * C_QK_PAD + C_NOPE_DIM:(h + 1) * C_QK_PAD] = kr
    v_ref[...] = jnp.dot(ckvn, wv_ref[...], preferred_element_type=F32).astype(v_ref.dtype)


def mla_up_call(proj, posf, gq, gkv, freqs, wq, wqp, wk, wv, tm=512):
    m = proj.shape[0]
    tm = min(tm, m)
    const = lambda shape: pl.BlockSpec(shape, lambda i: (0, 0))
    return pl.pallas_call(
        _mla_up_kernel,
        grid=(m // tm,),
        in_specs=[
            pl.BlockSpec((tm, C_RANK), lambda i: (i, OFF_CQ // C_RANK)),
            pl.BlockSpec((tm, C_RANK), lambda i: (i, OFF_CKV // C_RANK)),
            pl.BlockSpec((tm, 2 * LANES), lambda i: (i, OFF_CKR // (2 * LANES))),
            pl.BlockSpec((tm, 1), lambda i: (i, 0)),
            const((1, C_RANK)), const((1, C_RANK)), const((1, LANES)),
            const(wq.shape), const(wqp.shape), const(wk.shape), const(wv.shape),
        ],
        out_specs=[
            pl.BlockSpec((tm, HEADS * C_QK_PAD), lambda i: (i, 0)),
            pl.BlockSpec((tm, HEADS * C_QK_PAD), lambda i: (i, 0)),
            pl.BlockSpec((tm, HEADS * HEAD_DIM), lambda i: (i, 0)),
        ],
        out_shape=[
            jax.ShapeDtypeStruct((m, HEADS * C_QK_PAD), BF16),
            jax.ShapeDtypeStruct((m, HEADS * C_QK_PAD), BF16),
            jax.ShapeDtypeStruct((m, HEADS * HEAD_DIM), BF16),
        ],
        compiler_params=_cparams(("parallel",)),
        name="mla_up",
    )(proj, proj, proj, posf, gq, gkv, freqs, wq, wqp, wk, wv)


def _fox_cumsum_kernel(f_ref, b_ref, o_ref, *, chunk):
    s_len = f_ref.shape[0]
    r = lax.broadcasted_iota(jnp.int32, (chunk, chunk), 0)
    c = lax.broadcasted_iota(jnp.int32, (chunk, chunk), 1)
    tri = (c <= r).astype(F32)
    carry = jnp.zeros((1, LANES), F32)
    for i in range(s_len // chunk):
        x = f_ref[i * chunk:(i + 1) * chunk, :].astype(F32) + b_ref[...]
        ls = jnp.minimum(x, 0.0) - jnp.log(1.0 + jnp.exp(-jnp.abs(x)))
        cum = jnp.dot(tri, ls, preferred_element_type=F32, precision=lax.Precision.HIGHEST) + carry
        o_ref[i * chunk:(i + 1) * chunk, :] = cum
        carry = cum[chunk - 1:chunk, :]


def fox_cumsum_call(proj3, bias_row):
    b, s_len, _ = proj3.shape
    chunk = min(512, s_len)
    return pl.pallas_call(
        functools.partial(_fox_cumsum_kernel, chunk=chunk),
        grid=(b,),
        in_specs=[pl.BlockSpec((None, s_len, LANES), lambda i: (i, 0, OFF_BF // LANES)),
                  pl.BlockSpec((1, LANES), lambda i: (0, 0))],
        out_specs=pl.BlockSpec((None, s_len, LANES), lambda i: (i, 0, 0)),
        out_shape=jax.ShapeDtypeStruct((b, s_len, LANES), F32),
        compiler_params=_cparams(("parallel",)),
        name="fox_cumsum",
    )(proj3, bias_row)


def _flash_loop(qt_scr, k_ref, v_ref, m_scr, l_scr, acc_scr, gap_scr, *, q0, tq, tk, bias_fn, causal_chunked):
    ncols = qt_scr.shape[1]

    def init():
        m_scr[...] = jnp.full(m_scr.shape, NEG_INF, F32)
        l_scr[...] = jnp.zeros(l_scr.shape, F32)
        acc_scr[...] = jnp.zeros(acc_scr.shape, F32)
        gap_scr[...] = jnp.full(gap_scr.shape, NEG_INF, F32)

    def scores(j, masked, c0):
        w = ncols - c0
        ks = pl.multiple_of(j * tk, tk)
        s = jnp.dot(k_ref[pl.ds(ks, tk), :], qt_scr[:, c0:], preferred_element_type=F32)
        if masked and bias_fn is not None:
            s = bias_fn(s, ks, c0, w)
        if masked:
            kidx = ks + lax.broadcasted_iota(jnp.int32, (tk, w), 0)
            qidx = q0 + ((c0 + lax.broadcasted_iota(jnp.int32, (tk, w), 1)) & (tq - 1))
            if causal_chunked:
                keep = (kidx >> CHUNK_SHIFT) <= (qidx >> CHUNK_SHIFT)
            else:
                keep = kidx <= qidx
            s = jnp.where(keep, s, NEG_INF)
        return s

    def pv_of(p, j):
        ks = pl.multiple_of(j * tk, tk)
        return lax.dot_general(v_ref[pl.ds(ks, tk), :], p.astype(BF16), (((0,), (0,)), ((), ())),
                               preferred_element_type=F32)

    def update_exact(s, j, c0):
        cols = slice(c0, ncols)
        m_prev = m_scr[:, cols]
        m_new = jnp.maximum(m_prev, jnp.max(s, axis=0, keepdims=True))
        alpha = jnp.exp2(m_prev - m_new)
        p = jnp.exp2(s - m_new)
        l_scr[:, cols] = alpha * l_scr[:, cols] + jnp.sum(p, axis=0, keepdims=True)
        acc_scr[:, cols] = alpha * acc_scr[:, cols] + pv_of(p, j)
        m_scr[:, cols] = m_new

    def update_lagged(s, j, c0):
        cols = slice(c0, ncols)
        m_prev = m_scr[:, cols]
        p = jnp.exp2(s - m_prev)
        t_max = jnp.max(s, axis=0, keepdims=True)
        m_new = jnp.maximum(m_prev, t_max)
        beta = jnp.exp2(m_prev - m_new)
        l_scr[:, cols] = (l_scr[:, cols] + jnp.sum(p, axis=0, keepdims=True)) * beta
        acc_scr[:, cols] = (acc_scr[:, cols] + pv_of(p, j)) * beta
        gap_scr[:, cols] = jnp.maximum(gap_scr[:, cols], t_max - m_prev)
        m_scr[:, cols] = m_new

    def steps(tiles, update):
        s_all = [scores(j, masked, c0) for j, masked, c0 in tiles]
        for s, (j, _, c0) in zip(s_all, tiles):
            update(s, j, c0)

    n_full = q0 // tk
    single_map = ncols == tq
    diag = [(n_full + d, True, d * tk if single_map else 0) for d in range(tq // tk)]

    init()
    steps(diag[:1], update_exact)

    def body(i, carry):
        steps([(2 * i, False, 0), (2 * i + 1, False, 0)], update_lagged)
        return carry

    lax.fori_loop(0, n_full // 2, body, 0)
    if (tq // tk) % 2:

        @pl.when(n_full % 2 == 1)
        def _():
            steps([(n_full - 1, False, 0)], update_lagged)

    for i in range(1, len(diag), 2):
        steps(diag[i:i + 2], update_lagged)

    @pl.when(jnp.max(gap_scr[...]) > MAX_GAP)
    def _():
        init()

        def body_exact(j, carry):
            steps([(j, False, 0)], update_exact)
            return carry

        lax.fori_loop(0, n_full, body_exact, 0)
        for tile in diag:
            steps([tile], update_exact)


def _lane_tile(x, ncols):
    return x if ncols == x.shape[1] else jnp.concatenate([x] * (ncols // x.shape[1]), axis=1)


LOG2E = math.log2(math.e)
MAX_GAP = 60.0
N_AUG = 3


def _bf16_pieces(x):
    hi = x.astype(BF16).astype(F32)
    rest = x - hi
    mid = rest.astype(BF16).astype(F32)
    return hi, mid, rest - mid


def _aug_key_cols(col):
    hi, mid, lo = _bf16_pieces(col)
    lane = lax.broadcasted_iota(jnp.int32, col.shape, 1)
    ones = jnp.where(lane < 2 * N_AUG, 1.0, 0.0)
    return jnp.where(lane == 0, hi, jnp.where(lane == 1, mid, jnp.where(lane == 2, lo, ones))).astype(BF16)


def _aug_query_rows(row_vec, n_rows):
    hi, mid, lo = _bf16_pieces(jnp.broadcast_to(row_vec, (n_rows, row_vec.shape[1])))
    row = lax.broadcasted_iota(jnp.int32, hi.shape, 0)
    ones = jnp.where(row < N_AUG, 1.0, 0.0)
    return jnp.where(row == 3, hi, jnp.where(row == 4, mid, jnp.where(row == 5, lo, ones))).astype(BF16)


def _attn_a_kernel(slopes_ref, q_ref, k_ref, v_ref, posq_ref, posk_ref, lam_ref, g_ref, o_ref,
                   kcat_scr, qt_scr, m_scr, l_scr, acc_scr, gap_scr, *, tq, tk, lam_init):
    h = pl.program_id(1)
    q0 = pl.program_id(2) * tq
    q_t = q_ref[...].astype(F32).T
    row = lax.broadcasted_iota(jnp.int32, q_t.shape, 0)
    zero = jnp.zeros_like(q_t)
    qt_scr[:HEAD_DIM, :tq] = jnp.where(row < A_QK_DIM, q_t, zero).astype(BF16)
    qt_scr[:HEAD_DIM, tq:] = jnp.where(row >= A_QK_DIM, q_t, zero).astype(BF16)
    slope = slopes_ref[h] * LOG2E
    pq = _lane_tile(posq_ref[...], 2 * tq)
    qt_scr[HEAD_DIM:, :] = _aug_query_rows(pq * (-slope), HEAD_DIM)

    @pl.when(pl.program_id(2) == 0)
    def _():
        kcat_scr[:, :HEAD_DIM] = k_ref[...]
        kcat_scr[:, HEAD_DIM:] = _aug_key_cols(posk_ref[...] * slope)

    def bias_fn(s, ks, c0, w):
        pk = _lane_tile(posk_ref[pl.ds(ks, tk), :], w)
        return s + (2.0 * slope) * jnp.minimum(pq[:, c0:c0 + w] - pk, 0.0)

    _flash_loop(qt_scr, kcat_scr, v_ref, m_scr, l_scr, acc_scr, gap_scr, q0=q0, tq=tq, tk=tk,
                bias_fn=bias_fn, causal_chunked=True)
    dl = lam_ref[...]
    lam = (jnp.exp(jnp.sum(dl[0:1] * dl[1:2], axis=1, keepdims=True))
           - jnp.exp(jnp.sum(dl[2:3] * dl[3:4], axis=1, keepdims=True)) + lam_init)
    accn = acc_scr[...] * (1.0 / l_scr[...])
    o = (accn[:, :tq] - lam * accn[:, tq:]).T
    o_ref[...] = (_rms(o, g_ref[...]) * (1.0 - lam_init)).astype(o_ref.dtype)


def attn_a_call(proj3, posq_row, posk_rep, slopes, diff_lambda, subln_g, lam_init, tq=512, tk=512):
    b, s_len, _ = proj3.shape
    tq = min(tq, s_len)
    tk = min(tk, tq)
    assert tq % tk == 0 and s_len % tq == 0
    kern = functools.partial(_attn_a_kernel, tq=tq, tk=tk, lam_init=lam_init)
    cb = OFF_A // HEAD_DIM
    return pl.pallas_call(
        kern,
        grid=(b, HEADS, s_len // tq),
        in_specs=[
            pl.BlockSpec(memory_space=pltpu.SMEM),
            pl.BlockSpec((None, tq, HEAD_DIM), lambda bi, h, i: (bi, i, cb + h)),
            pl.BlockSpec((None, s_len, HEAD_DIM), lambda bi, h, i: (bi, 0, cb + HEADS + h)),
            pl.BlockSpec((None, s_len, HEAD_DIM), lambda bi, h, i: (bi, 0, cb + 2 * HEADS + h)),
            pl.BlockSpec((None, 1, tq), lambda bi, h, i: (bi, 0, i)),
            pl.BlockSpec((None, s_len, LANES), lambda bi, h, i: (bi, 0, 0)),
            pl.BlockSpec((4, A_QK_DIM), lambda bi, h, i: (0, 0)),
            pl.BlockSpec((1, HEAD_DIM), lambda bi, h, i: (0, 0)),
        ],
        out_specs=pl.BlockSpec((None, tq, HEAD_DIM), lambda bi, h, i: (bi, i, h)),
        out_shape=jax.ShapeDtypeStruct((b, s_len, HEADS * HEAD_DIM), BF16),
        scratch_shapes=[pltpu.VMEM((s_len, 2 * HEAD_DIM), BF16), pltpu.VMEM((2 * HEAD_DIM, 2 * tq), BF16),
                        pltpu.VMEM((1, 2 * tq), F32),
                        pltpu.VMEM((1, 2 * tq), F32), pltpu.VMEM((HEAD_DIM, 2 * tq), F32),
                        pltpu.VMEM((1, 2 * tq), F32)],
        compiler_params=_cparams(("parallel", "parallel", "arbitrary")),
        name="attn_diff",
    )(slopes, proj3, proj3, proj3, posq_row, posk_rep, diff_lambda, subln_g.reshape(1, HEAD_DIM))


def _attn_b_kernel(q_ref, k_ref, v_ref, cum_ref, o_ref, kcat_scr, qt_scr, m_scr, l_scr, acc_scr, gap_scr, *, tq, tk):
    h = pl.program_id(1)
    qi = pl.program_id(2)

    @pl.when(qi == 0)
    def _():
        lane = lax.broadcasted_iota(jnp.int32, cum_ref.shape, 1)
        col = jnp.sum(jnp.where(lane == h, cum_ref[...], 0.0), axis=1, keepdims=True)
        kcat_scr[:, :HEAD_DIM] = k_ref[...]
        kcat_scr[:, HEAD_DIM:] = _aug_key_cols(jnp.broadcast_to(col * (-LOG2E), cum_ref.shape))

    qt_scr[:HEAD_DIM, :] = q_ref[...].astype(F32).T.astype(BF16)
    cum_t = cum_ref[pl.ds(pl.multiple_of(qi * tq, tq), tq), :].T
    head = lax.broadcasted_iota(jnp.int32, cum_t.shape, 0)
    cq = jnp.sum(jnp.where(head == h, cum_t, 0.0), axis=0, keepdims=True)
    qt_scr[HEAD_DIM:, :] = _aug_query_rows(cq * LOG2E, HEAD_DIM)
    _flash_loop(qt_scr, kcat_scr, v_ref, m_scr, l_scr, acc_scr, gap_scr, q0=qi * tq, tq=tq, tk=tk,
                bias_fn=None, causal_chunked=False)
    o_ref[...] = (acc_scr[...] * (1.0 / l_scr[...])).T.astype(o_ref.dtype)


def attn_b_call(proj3, cum, tq=1024, tk=512):
    b, s_len, _ = proj3.shape
    tq = min(tq, s_len)
    tk = min(tk, tq)
    assert tq % tk == 0 and s_len % tq == 0
    cb = OFF_B // HEAD_DIM
    return pl.pallas_call(
        functools.partial(_attn_b_kernel, tq=tq, tk=tk),
        grid=(b, HEADS, s_len // tq),
        in_specs=[
            pl.BlockSpec((None, tq, HEAD_DIM), lambda bi, h, i: (bi, i, cb + h)),
            pl.BlockSpec((None, s_len, HEAD_DIM), lambda bi, h, i: (bi, 0, cb + HEADS + h)),
            pl.BlockSpec((None, s_len, HEAD_DIM), lambda bi, h, i: (bi, 0, cb + 2 * HEADS + h)),
            pl.BlockSpec((None, s_len, LANES), lambda bi, h, i: (bi, 0, 0)),
        ],
        out_specs=pl.BlockSpec((None, tq, HEAD_DIM), lambda bi, h, i: (bi, i, h)),
        out_shape=jax.ShapeDtypeStruct((b, s_len, HEADS * HEAD_DIM), BF16),
        scratch_shapes=[pltpu.VMEM((s_len, 2 * HEAD_DIM), BF16), pltpu.VMEM((2 * HEAD_DIM, tq), BF16),
                        pltpu.VMEM((1, tq), F32), pltpu.VMEM((1, tq), F32), pltpu.VMEM((HEAD_DIM, tq), F32),
                        pltpu.VMEM((1, tq), F32)],
        compiler_params=_cparams(("parallel", "parallel", "arbitrary")),
        name="attn_fox",
    )(proj3, proj3, proj3, cum)


def _attn_c_kernel(q_ref, k_ref, v_ref, o_ref, qt_scr, m_scr, l_scr, acc_scr, gap_scr, *, tq, tk):
    qt_scr[...] = q_ref[...].astype(F32).T.astype(BF16)
    _flash_loop(qt_scr, k_ref, v_ref, m_scr, l_scr, acc_scr, gap_scr,q0=pl.program_id(2) * tq, tq=tq, tk=tk,
                bias_fn=None, causal_chunked=True)
    o_ref[...] = (acc_scr[...] * (1.0 / l_scr[...])).T.astype(o_ref.dtype)


def attn_c_call(q3, k3, v3, tq=1024, tk=512):
    b, s_len, _ = q3.shape
    tq = min(tq, s_len)
    tk = min(tk, tq)
    assert tq % tk == 0 and s_len % tq == 0
    return pl.pallas_call(
        functools.partial(_attn_c_kernel, tq=tq, tk=tk),
        grid=(b, HEADS, s_len // tq),
        in_specs=[
            pl.BlockSpec((None, tq, C_QK_PAD), lambda bi, h, i: (bi, i, h)),
            pl.BlockSpec((None, s_len, C_QK_PAD), lambda bi, h, i: (bi, 0, h)),
            pl.BlockSpec((None, s_len, HEAD_DIM), lambda bi, h, i: (bi, 0, h)),
        ],
        out_specs=pl.BlockSpec((None, tq, HEAD_DIM), lambda bi, h, i: (bi, i, h)),
        out_shape=jax.ShapeDtypeStruct((b, s_len, HEADS * HEAD_DIM), BF16),
        scratch_shapes=[pltpu.VMEM((C_QK_PAD, tq), BF16), pltpu.VMEM((1, tq), F32), pltpu.VMEM((1, tq), F32),
                        pltpu.VMEM((HEAD_DIM, tq), F32), pltpu.VMEM((1, tq), F32)],
        compiler_params=_cparams(("parallel", "parallel", "arbitrary")),
        name="attn_mla",
    )(q3, k3, v3)


def _merge_kernel(x_ref, oa_ref, ob_ref, oc_ref, ga_ref, gb_ref, gc_ref, wb_ref, wo_ref, g_ref,
                  x_out_ref, h_out_ref):
    merged = None
    for gi, (o_ref, gate_ref) in enumerate(((oa_ref, ga_ref), (ob_ref, gb_ref), (oc_ref, gc_ref))):
        br = jnp.dot(o_ref[...], wb_ref[gi], preferred_element_type=F32)
        term = jax.nn.sigmoid(gate_ref[...].astype(F32)) * br
        merged = term if merged is None else merged + term
    x_new = x_ref[...] + jnp.dot(merged.astype(BF16), wo_ref[...], preferred_element_type=F32)
    x_out_ref[...] = x_new
    h_out_ref[...] = _rms(x_new, g_ref[...]).astype(h_out_ref.dtype)


def merge_call(x, o_a, o_b, o_c, proj, w_branch, w_out, norm_g, tm=256):
    m, d = x.shape
    tm = min(tm, m)
    row = lambda width: pl.BlockSpec((tm, width), lambda i: (i, 0))
    gate = lambda gi: pl.BlockSpec((tm, d), lambda i: (i, OFF_GATES // d + gi))
    return pl.pallas_call(
        _merge_kernel,
        grid=(m // tm,),
        in_specs=[
            row(d), row(BRANCH_WIDTH), row(BRANCH_WIDTH), row(BRANCH_WIDTH), gate(0), gate(1), gate(2),
            pl.BlockSpec(w_branch.shape, lambda i: (0, 0, 0), pipeline_mode=pl.Buffered(1)),
            pl.BlockSpec(w_out.shape, lambda i: (0, 0), pipeline_mode=pl.Buffered(1)),
            pl.BlockSpec((1, d), lambda i: (0, 0)),
        ],
        out_specs=[row(d), row(d)],
        out_shape=[jax.ShapeDtypeStruct((m, d), F32), jax.ShapeDtypeStruct((m, d), BF16)],
        compiler_params=_cparams(("parallel",)),
        name="merge_out",
    )(x, o_a, o_b, o_c, proj, proj, proj, w_branch, w_out, norm_g.reshape(1, d))


def _ffn_kernel(h_ref, x_ref, w1_ref, w3_ref, w2_ref, g_ref, *outs, emit_x):
    acc_ref = outs[-1]
    f = pl.program_id(1)

    @pl.when(f == 0)
    def _():
        acc_ref[...] = x_ref[...]

    h = h_ref[...]
    a = jnp.dot(h, w1_ref[...], preferred_element_type=F32)
    b = jnp.dot(h, w3_ref[...], preferred_element_type=F32)
    act = (a * jax.nn.sigmoid(a) * b).astype(BF16)
    acc_ref[...] += jnp.dot(act, w2_ref[...], preferred_element_type=F32)

    @pl.when(f == pl.num_programs(1) - 1)
    def _():
        x_new = acc_ref[...]
        if emit_x:
            outs[0][...] = x_new
        outs[-2][...] = _rms(x_new, g_ref[...]).astype(outs[-2].dtype)


def ffn_call(h, x, w1, w3, w2, norm_g, *, emit_x, norm_dtype, tm=512, tf=512):
    m, d = x.shape
    tm = min(tm, m)
    row = pl.BlockSpec((tm, d), lambda i, f: (i, 0))
    w_in = pl.BlockSpec((d, tf), lambda i, f: (0, f))
    out_specs = [row]
    out_shape = [jax.ShapeDtypeStruct((m, d), norm_dtype)]
    if emit_x:
        out_specs = [row, row]
        out_shape = [jax.ShapeDtypeStruct((m, d), F32)] + out_shape
    return pl.pallas_call(
        functools.partial(_ffn_kernel, emit_x=emit_x),
        grid=(m // tm, w1.shape[1] // tf),
        in_specs=[row, row, w_in, w_in, pl.BlockSpec((tf, d), lambda i, f: (f, 0)),
                  pl.BlockSpec((1, d), lambda i, f: (0, 0))],
        out_specs=out_specs,
        out_shape=out_shape,
        scratch_shapes=[pltpu.VMEM((tm, d), F32)],
        compiler_params=_cparams(("parallel", "arbitrary")),
        name="dense_ffn",
    )(h, x, w1, w3, w2, norm_g.reshape(1, d))


def _router_kernel(x_ref, g_ref, w_ref, i_ref, c_ref):
    h = _rms(x_ref[...], g_ref[...])
    logits = jnp.dot(h, w_ref[...], preferred_element_type=F32, precision=lax.Precision.HIGHEST)
    lane = lax.broadcasted_iota(jnp.int32, logits.shape, 1)
    logits = jnp.where(lane < N_EXPERTS, logits, -jnp.inf)
    m1 = jnp.max(logits, axis=1, keepdims=True)
    i1 = jnp.min(jnp.where(logits == m1, lane, LANES), axis=1, keepdims=True)
    rest = jnp.where(lane == i1, -jnp.inf, logits)
    m2 = jnp.max(rest, axis=1, keepdims=True)
    i2 = jnp.min(jnp.where(rest == m2, lane, LANES), axis=1, keepdims=True)
    e2 = jnp.exp(m2 - m1)
    inv = 1.0 / (1.0 + e2)
    i_ref[...] = jnp.where(lane == 0, i1, jnp.where(lane == 1, i2, 0))
    c_ref[...] = jnp.where(lane == 0, inv, jnp.where(lane == 1, e2 * inv, 0.0))


def router_call(x, norm_g, w_pad, tm=512):
    m, d = x.shape
    tm = min(tm, m)
    out = pl.BlockSpec((tm, LANES), lambda i: (i, 0))
    return pl.pallas_call(
        _router_kernel,
        grid=(m // tm,),
        in_specs=[pl.BlockSpec((tm, d), lambda i: (i, 0)), pl.BlockSpec((1, d), lambda i: (0, 0)),
                  pl.BlockSpec((d, LANES), lambda i: (0, 0))],
        out_specs=[out, out],
        out_shape=[jax.ShapeDtypeStruct((m, LANES), jnp.int32), jax.ShapeDtypeStruct((m, LANES), F32)],
        compiler_params=_cparams(("parallel",)),
        name="router",
    )(x, norm_g.reshape(1, d), w_pad)


def _row_copy(src_hbm, idx_ref, dst_ref, sem, r):
    return pltpu.make_async_copy(src_hbm.at[pl.ds(idx_ref[0, r], 1)], dst_ref.at[pl.ds(r, 1)], sem)


def _gather_rows(src_hbm, idx_ref, dst_ref, sem):
    n = dst_ref.shape[0]

    def start(r, c):
        _row_copy(src_hbm, idx_ref, dst_ref, sem, r).start()
        return c

    def wait(r, c):
        _row_copy(src_hbm, idx_ref, dst_ref, sem, r).wait()
        return c

    lax.fori_loop(0, n, start, 0, unroll=8)
    lax.fori_loop(0, n, wait, 0, unroll=8)


def _gather_kernel(idx_ref, src_hbm, o_ref, sem):
    _gather_rows(src_hbm, idx_ref, o_ref, sem)


def gather_rows_call(src, idx, tg=256):
    n_rows = idx.shape[0]
    width = src.shape[1]
    return pl.pallas_call(
        _gather_kernel,
        grid=(n_rows // tg,),
        in_specs=[pl.BlockSpec((None, 1, tg), lambda i: (i, 0, 0), memory_space=pltpu.SMEM),
                  pl.BlockSpec(memory_space=pl.ANY)],
        out_specs=pl.BlockSpec((tg, width), lambda i: (i, 0)),
        out_shape=jax.ShapeDtypeStruct((n_rows, width), src.dtype),
        scratch_shapes=[pltpu.SemaphoreType.DMA],
        compiler_params=_cparams(("arbitrary",)),
        name="moe_gather",
    )(idx.reshape(n_rows // tg, 1, tg), src)


def _moe_group_kernel(te_ref, nv_ref, xs_ref, g_ref, w1_ref, w3_ref, w2_ref, o_ref, h_scr, acc_scr):
    del te_ref
    f = pl.program_id(1)
    valid = pl.program_id(0) < nv_ref[0]
    last = f == pl.num_programs(1) - 1

    @pl.when(valid & (f == 0))
    def _():
        h_scr[...] = _rms(xs_ref[...], g_ref[...]).astype(h_scr.dtype)
        acc_scr[...] = jnp.zeros(acc_scr.shape, F32)

    @pl.when(valid)
    def _():
        h = h_scr[...]
        a = jnp.dot(h, w1_ref[...], preferred_element_type=F32)
        b = jnp.dot(h, w3_ref[...], preferred_element_type=F32)
        act = (a * jax.nn.sigmoid(a) * b).astype(BF16)
        acc_scr[...] += jnp.dot(act, w2_ref[...], preferred_element_type=F32)

    @pl.when(valid & last)
    def _():
        o_ref[...] = acc_scr[...]

    @pl.when(jnp.logical_not(valid) & last)
    def _():
        o_ref[...] = jnp.zeros(o_ref.shape, o_ref.dtype)


def moe_group_call(xs, norm_g, w1, w3, w2, tile_expert, n_valid, tm, tf=512):
    mp, d = xs.shape
    n_f = w1.shape[-1] // tf

    def f_sel(i, f, nv):
        return jnp.where(i < nv[0], f, n_f - 1)

    w_in = pl.BlockSpec((None, d, tf), lambda i, f, te, nv: (te[i], 0, f_sel(i, f, nv)))
    w_out = pl.BlockSpec((None, tf, d), lambda i, f, te, nv: (te[i], f_sel(i, f, nv), 0))
    row = pl.BlockSpec((tm, d), lambda i, f, te, nv: (i, 0))
    return pl.pallas_call(
        _moe_group_kernel,
        grid_spec=pltpu.PrefetchScalarGridSpec(
            num_scalar_prefetch=2,
            grid=(mp // tm, n_f),
            in_specs=[row, pl.BlockSpec((1, d), lambda i, f, te, nv: (0, 0)), w_in, w_in, w_out],
            out_specs=row,
            scratch_shapes=[pltpu.VMEM((tm, d), BF16), pltpu.VMEM((tm, d), F32)],
        ),
        out_shape=jax.ShapeDtypeStruct((mp, d), F32),
        compiler_params=_cparams(("arbitrary", "arbitrary")),
        name="moe_experts",
    )(tile_expert, n_valid, xs, norm_g.reshape(1, d), w1, w3, w2)


def _moe_combine_kernel(p0_ref, p1_ref, x_ref, c_ref, g_ref, ys_hbm, *rest, emit_x):
    *outs, y0_scr, y1_scr, sem0, sem1 = rest
    n = x_ref.shape[0]

    def start(r, c):
        _row_copy(ys_hbm, p0_ref, y0_scr, sem0, r).start()
        _row_copy(ys_hbm, p1_ref, y1_scr, sem1, r).start()
        return c

    def wait(r, c):
        _row_copy(ys_hbm, p0_ref, y0_scr, sem0, r).wait()
        _row_copy(ys_hbm, p1_ref, y1_scr, sem1, r).wait()
        return c

    lax.fori_loop(0, n, start, 0, unroll=8)
    lax.fori_loop(0, n, wait, 0, unroll=8)
    lane = lax.broadcasted_iota(jnp.int32, c_ref.shape, 1)
    c = c_ref[...]
    c0 = jnp.sum(jnp.where(lane == 0, c, 0.0), axis=1, keepdims=True)
    c1 = jnp.sum(jnp.where(lane == 1, c, 0.0), axis=1, keepdims=True)
    x_new = x_ref[...] + c0 * y0_scr[...] + c1 * y1_scr[...]
    if emit_x:
        outs[0][...] = x_new
    outs[-1][...] = _rms(x_new, g_ref[...]).astype(outs[-1].dtype)


def moe_combine_call(x, weights, ys, pos0, pos1, norm_g, *, emit_x, norm_dtype, tg=256):
    m, d = x.shape
    tg = min(tg, m)
    idx = pl.BlockSpec((None, 1, tg), lambda i: (i, 0, 0), memory_space=pltpu.SMEM)
    row = pl.BlockSpec((tg, d), lambda i: (i, 0))
    out_specs = [row]
    out_shape = [jax.ShapeDtypeStruct((m, d), norm_dtype)]
    if emit_x:
        out_specs = [row, row]
        out_shape = [jax.ShapeDtypeStruct((m, d), F32)] + out_shape
    return pl.pallas_call(
        functools.partial(_moe_combine_kernel, emit_x=emit_x),
        grid=(m // tg,),
        in_specs=[idx, idx, row, pl.BlockSpec((tg, LANES), lambda i: (i, 0)),
                  pl.BlockSpec((1, d), lambda i: (0, 0)), pl.BlockSpec(memory_space=pl.ANY)],
        out_specs=out_specs,
        out_shape=out_shape,
        scratch_shapes=[pltpu.VMEM((tg, d), F32), pltpu.VMEM((tg, d), F32),
                        pltpu.SemaphoreType.DMA, pltpu.SemaphoreType.DMA],
        compiler_params=_cparams(("arbitrary",)),
        name="moe_combine",
    )(pos0.reshape(m // tg, 1, tg), pos1.reshape(m // tg, 1, tg), x, weights, norm_g.reshape(1, d), ys)


def _route(choice, tm):
    t = choice.shape[0]
    n_tiles = (TOP_K * t) // tm + N_EXPERTS
    e = choice.reshape(-1)
    onehot = (e[:, None] == jnp.arange(N_EXPERTS, dtype=jnp.int32)[None, :]).astype(jnp.int32)
    rank = jnp.cumsum(onehot, axis=0) - onehot
    counts = jnp.sum(onehot, axis=0)
    tiles_per = (counts + tm - 1) // tm
    tile_end = jnp.cumsum(tiles_per)
    start_row = (tile_end - tiles_per) * tm
    pos = jnp.sum(onehot * (start_row[None, :] + rank), axis=1)
    src = jnp.zeros((n_tiles * tm,), jnp.int32).at[pos].set(jnp.arange(TOP_K * t, dtype=jnp.int32) // TOP_K)
    n_valid = tile_end[-1:]
    tile_id = jnp.minimum(jnp.arange(n_tiles, dtype=jnp.int32), n_valid[0] - 1)
    tile_expert = jnp.sum((tile_id[:, None] >= tile_end[None, :]).astype(jnp.int32), axis=1)
    return pos.reshape(t, TOP_K), src, tile_expert.astype(jnp.int32), n_valid.astype(jnp.int32)


def _rot_pair(w):
    half = w.shape[-1] // 2
    return jnp.concatenate([-w[..., half:], w[..., :half]], axis=-1)


def _pack_in_proj(w):
    d = w.shape[0]
    bw = BRANCH_WIDTH
    aq, ak, av, bq, bk, bv = (w[:, i * bw:(i + 1) * bw] for i in range(6))
    o = 6 * bw
    bf = w[:, o:o + HEADS]
    o += HEADS
    cq = w[:, o:o + C_RANK]
    ckv = w[:, o + C_RANK:o + 2 * C_RANK]
    o += 2 * C_RANK
    ckr = w[:, o:o + C_ROPE_DIM]
    gates = w[:, o + C_ROPE_DIM:]
    z = lambda n: jnp.zeros((d, n), w.dtype)
    packed = jnp.concatenate(
        [gates, aq * (A_QK_DIM ** -0.5 * LOG2E), ak, av, bq * (HEAD_DIM ** -0.5 * LOG2E), bk, bv, cq, ckv,
         ckr, z(LANES - C_ROPE_DIM), _rot_pair(ckr), z(LANES - C_ROPE_DIM),
         bf, z(LANES - HEADS), z(N_PROJ - OFF_BF - LANES)], axis=1)
    return packed.astype(BF16)


def _pack_mla(w_uq, w_uk, w_uv):
    r = w_uq.shape[0]
    scale = (C_NOPE_DIM + C_ROPE_DIM) ** -0.5 * LOG2E
    nope, rope = w_uq[..., :C_NOPE_DIM] * scale, w_uq[..., C_NOPE_DIM:] * scale
    zpad = jnp.zeros((r, HEADS, C_QK_PAD - C_NOPE_DIM - C_ROPE_DIM), w_uq.dtype)
    wq = jnp.concatenate([nope, rope, zpad], axis=-1).reshape(r, HEADS * C_QK_PAD)
    wqp = jnp.concatenate([_rot_pair(rope), zpad], axis=-1).reshape(r, HEADS * LANES)
    return (wq.astype(BF16), wqp.astype(BF16), w_uk.reshape(r, -1).astype(BF16),
            w_uv.reshape(r, -1).astype(BF16))


def _rope_freqs():
    half = C_ROPE_DIM // 2
    inv = ROPE_THETA ** (-jnp.arange(half, dtype=F32) / half)
    return jnp.concatenate([inv, inv, jnp.zeros((LANES - C_ROPE_DIM,), F32)]).reshape(1, LANES)


def kernel(x, positions, attn_norm_g, w_in, diff_lambda, diff_subln_g, fox_forget_bias, mla_q_norm_g, mla_kv_norm_g, mla_w_uq, mla_w_uk, mla_w_uv, w_branch, w_out, ffn_norm_g, dense_w1, dense_w3, dense_w2, router_w, expert_w1, expert_w3, expert_w2, final_norm_g):
    b, s_len, d = x.shape
    m = b * s_len
    depth = w_in.shape[0]
    posf = positions.astype(F32)
    posq_row = posf.reshape(b, 1, s_len)
    posk_rep = jnp.broadcast_to(posf[:, :, None], (b, s_len, LANES))
    pos_col = posf.reshape(m, 1)
    slopes = jnp.exp2(-8.0 * (jnp.arange(HEADS, dtype=F32) + 1.0) / HEADS)
    freqs = _rope_freqs()

    xr = x.reshape(m, d)
    hn = rmsnorm_call(xr, attn_norm_g[0], BF16)
    out = None
    for l in range(depth):
        lam_init = 0.8 - 0.6 * math.exp(-0.3 * l)
        proj = matmul_call(hn, _pack_in_proj(w_in[l]), BF16)
        proj3 = proj.reshape(b, s_len, N_PROJ)

        o_a = attn_a_call(proj3, posq_row, posk_rep, slopes, diff_lambda[l], diff_subln_g[l], lam_init)
        bias_row = jnp.zeros((1, LANES), F32).at[0, :HEADS].set(fox_forget_bias[l])
        o_b = attn_b_call(proj3, fox_cumsum_call(proj3, bias_row))
        wq, wqp, wk, wv = _pack_mla(mla_w_uq[l], mla_w_uk[l], mla_w_uv[l])
        q_c, k_c, v_c = mla_up_call(proj, pos_col, mla_q_norm_g[l].reshape(1, -1),
                                    mla_kv_norm_g[l].reshape(1, -1), freqs, wq, wqp, wk, wv)
        o_c = attn_c_call(q_c.reshape(b, s_len, -1), k_c.reshape(b, s_len, -1), v_c.reshape(b, s_len, -1))

        x1, h_ffn = merge_call(xr, o_a.reshape(m, -1), o_b.reshape(m, -1), o_c.reshape(m, -1), proj,
                               w_branch[l].astype(BF16), w_out[l].astype(BF16), ffn_norm_g[l])
        last = l == depth - 1
        next_g = final_norm_g if last else attn_norm_g[l + 1]
        j = l // 2
        if l % 2 == 0:
            res = ffn_call(h_ffn, x1, dense_w1[j].astype(BF16), dense_w3[j].astype(BF16),
                           dense_w2[j].astype(BF16), next_g, emit_x=not last,
                           norm_dtype=F32 if last else BF16)
        else:
            w_pad = jnp.zeros((d, LANES), F32).at[:, :N_EXPERTS].set(router_w[j])
            choice, weights = router_call(x1, ffn_norm_g[l], w_pad)
            tm_moe = min(512, m)
            pos, src, tile_expert, n_valid = _route(choice[:, :TOP_K], tm_moe)
            xs = gather_rows_call(x1, src)
            ys = moe_group_call(xs, ffn_norm_g[l], expert_w1[j].astype(BF16), expert_w3[j].astype(BF16),
                                expert_w2[j].astype(BF16), tile_expert, n_valid, tm_moe)
            res = moe_combine_call(x1, weights, ys, pos[:, 0], pos[:, 1], next_g, emit_x=not last,
                                   norm_dtype=F32 if last else BF16)
        if last:
            out = res[0]
        else:
            xr, hn = res
    return out.reshape(b, s_len, d)
```

```python
import functools
import math

import jax
import jax.numpy as jnp
from jax import lax
from jax.experimental import pallas as pl
from jax.experimental.pallas import tpu as pltpu

F32 = jnp.float32
BF16 = jnp.bfloat16

D_MODEL = 2048
CHUNK_SHIFT = 6
NORM_EPS = 1e-6
NEG_INF = -1e30

HEADS = 8
HEAD_DIM = 128
A_QK_DIM = 64
C_RANK = 512
C_NOPE_DIM = 128
C_ROPE_DIM = 64
C_QK_PAD = 256
ROPE_THETA = 10000.0
N_BRANCHES = 3
BRANCH_WIDTH = 1024
N_EXPERTS = 8
TOP_K = 2
LANES = 128

OFF_GATES = 0
OFF_A = 3 * D_MODEL
OFF_B = OFF_A + 3 * BRANCH_WIDTH
OFF_CQ = OFF_B + 3 * BRANCH_WIDTH
OFF_CKV = OFF_CQ + C_RANK
OFF_CKR = OFF_CKV + C_RANK
OFF_BF = OFF_CKR + 2 * LANES
N_PROJ = 13824

VMEM_LIMIT = 56 * 1024 * 1024


def _cparams(sem):
    return pltpu.CompilerParams(dimension_semantics=sem, vmem_limit_bytes=VMEM_LIMIT)


def _rms(x, g):
    return x * lax.rsqrt(jnp.mean(x * x, axis=-1, keepdims=True) + NORM_EPS) * g


def _rmsnorm_kernel(x_ref, g_ref, o_ref):
    o_ref[...] = _rms(x_ref[...].astype(F32), g_ref[...]).astype(o_ref.dtype)


def rmsnorm_call(x, g, out_dtype, tm=512):
    m, d = x.shape
    tm = min(tm, m)
    return pl.pallas_call(
        _rmsnorm_kernel,
        grid=(m // tm,),
        in_specs=[pl.BlockSpec((tm, d), lambda i: (i, 0)), pl.BlockSpec((1, d), lambda i: (0, 0))],
        out_specs=pl.BlockSpec((tm, d), lambda i: (i, 0)),
        out_shape=jax.ShapeDtypeStruct((m, d), out_dtype),
        compiler_params=_cparams(("parallel",)),
        name="rmsnorm",
    )(x, g.reshape(1, d))


def _matmul_kernel(a_ref, b_ref, o_ref):
    o_ref[...] = jnp.dot(a_ref[...], b_ref[...], preferred_element_type=F32).astype(o_ref.dtype)


def matmul_call(a, b, out_dtype, tm=1024, tn=1536):
    m, k = a.shape
    _, n = b.shape
    tm = min(tm, m)
    return pl.pallas_call(
        _matmul_kernel,
        grid=(m // tm, n // tn),
        in_specs=[pl.BlockSpec((tm, k), lambda i, j: (i, 0)), pl.BlockSpec((k, tn), lambda i, j: (0, j))],
        out_specs=pl.BlockSpec((tm, tn), lambda i, j: (i, j)),
        out_shape=jax.ShapeDtypeStruct((m, n), out_dtype),
        compiler_params=_cparams(("parallel", "arbitrary")),
        name="in_proj",
    )(a, b)


def _mla_up_kernel(cq_ref, ckv_ref, ckr_ref, pos_ref, gq_ref, gkv_ref, f_ref,
                   wq_ref, wqp_ref, wk_ref, wv_ref, q_ref, k_ref, v_ref):
    cqn = _rms(cq_ref[...].astype(F32), gq_ref[...]).astype(BF16)
    ckvn = _rms(ckv_ref[...].astype(F32), gkv_ref[...]).astype(BF16)
    ang = pos_ref[...] * f_ref[...]
    cos, sin = jnp.cos(ang), jnp.sin(ang)
    q = jnp.dot(cqn, wq_ref[...], preferred_element_type=F32)
    qp = jnp.dot(cqn, wqp_ref[...], preferred_element_type=F32)
    for h in range(HEADS):
        nope = slice(h * C_QK_PAD, h * C_QK_PAD + C_NOPE_DIM)
        rope = slice(h * C_QK_PAD + C_NOPE_DIM, (h + 1) * C_QK_PAD)
        q_ref[:, nope] = q[:, nope].astype(q_ref.dtype)
        q_ref[:, rope] = (q[:, rope] * cos + qp[:, h * LANES:(h + 1) * LANES] * sin).astype(q_ref.dtype)
    ckr = ckr_ref[...].astype(F32)
    kr = (ckr[:, :LANES] * cos + ckr[:, LANES:] * sin).astype(k_ref.dtype)
    kn = jnp.dot(ckvn, wk_ref[...], preferred_element_type=F32).astype(k_ref.dtype)
    for h in range(HEADS):
        k_ref[:, h * C_QK_PAD:h * C_QK_PAD + C_NOPE_DIM] = kn[:, h * C_NOPE_DIM:(h + 1) * C_NOPE_DIM]
        k_ref[:, h * C_QK_PAD + C_NOPE_DIM:(h + 1) * C_QK_PAD] = kr
    v_ref[...] = jnp.dot(ckvn, wv_ref[...], preferred_element_type=F32).astype(v_ref.dtype)


def mla_up_call(proj, posf, gq, gkv, freqs, wq, wqp, wk, wv, tm=512):
    m = proj.shape[0]
    tm = min(tm, m)
    const = lambda shape: pl.BlockSpec(shape, lambda i: (0, 0))
    return pl.pallas_call(
        _mla_up_kernel,
        grid=(m // tm,),
        in_specs=[
            pl.BlockSpec((tm, C_RANK), lambda i: (i, OFF_CQ // C_RANK)),
            pl.BlockSpec((tm, C_RANK), lambda i: (i, OFF_CKV // C_RANK)),
            pl.BlockSpec((tm, 2 * LANES), lambda i: (i, OFF_CKR // (2 * LANES))),
            pl.BlockSpec((tm, 1), lambda i: (i, 0)),
            const((1, C_RANK)), const((1, C_RANK)), const((1, LANES)),
            const(wq.shape), const(wqp.shape), const(wk.shape), const(wv.shape),
        ],
        out_specs=[
            pl.BlockSpec((tm, HEADS * C_QK_PAD), lambda i: (i, 0)),
            pl.BlockSpec((tm, HEADS * C_QK_PAD), lambda i: (i, 0)),
            pl.BlockSpec((tm, HEADS * HEAD_DIM), lambda i: (i, 0)),
        ],
        out_shape=[
            jax.ShapeDtypeStruct((m, HEADS * C_QK_PAD), BF16),
            jax.ShapeDtypeStruct((m, HEADS * C_QK_PAD), BF16),
            jax.ShapeDtypeStruct((m, HEADS * HEAD_DIM), BF16),
        ],
        compiler_params=_cparams(("parallel",)),
        name="mla_up",
    )(proj, proj, proj, posf, gq, gkv, freqs, wq, wqp, wk, wv)


def _fox_cumsum_kernel(f_ref, b_ref, o_ref, *, chunk):
    s_len = f_ref.shape[0]
    r = lax.broadcasted_iota(jnp.int32, (chunk, chunk), 0)
    c = lax.broadcasted_iota(jnp.int32, (chunk, chunk), 1)
    tri = (c <= r).astype(F32)
    carry = jnp.zeros((1, LANES), F32)
    for i in range(s_len // chunk):
        x = f_ref[i * chunk:(i + 1) * chunk, :].astype(F32) + b_ref[...]
        ls = jnp.minimum(x, 0.0) - jnp.log(1.0 + jnp.exp(-jnp.abs(x)))
        cum = jnp.dot(tri, ls, preferred_element_type=F32, precision=lax.Precision.HIGHEST) + carry
        o_ref[i * chunk:(i + 1) * chunk, :] = cum
        carry = cum[chunk - 1:chunk, :]


def fox_cumsum_call(proj3, bias_row):
    b, s_len, _ = proj3.shape
    chunk = min(512, s_len)
    return pl.pallas_call(
        functools.partial(_fox_cumsum_kernel, chunk=chunk),
        grid=(b,),
        in_specs=[pl.BlockSpec((None, s_len, LANES), lambda i: (i, 0, OFF_BF // LANES)),
                  pl.BlockSpec((1, LANES), lambda i: (0, 0))],
        out_specs=pl.BlockSpec((None, s_len, LANES), lambda i: (i, 0, 0)),
        out_shape=jax.ShapeDtypeStruct((b, s_len, LANES), F32),
        compiler_params=_cparams(("parallel",)),
        name="fox_cumsum",
    )(proj3, bias_row)


def _flash_loop(qt_scr, k_ref, v_ref, m_scr, l_scr, acc_scr, gap_scr, *, q0, tq, tk, bias_fn, causal_chunked):
    ncols = qt_scr.shape[1]

    def init():
        m_scr[...] = jnp.full(m_scr.shape, NEG_INF, F32)
        l_scr[...] = jnp.zeros(l_scr.shape, F32)
        acc_scr[...] = jnp.zeros(acc_scr.shape, F32)
        gap_scr[...] = jnp.full(gap_scr.shape, NEG_INF, F32)

    def scores(j, masked, c0):
        w = ncols - c0
        ks = pl.multiple_of(j * tk, tk)
        s = jnp.dot(k_ref[pl.ds(ks, tk), :], qt_scr[:, c0:], preferred_element_type=F32)
        if masked and bias_fn is not None:
            s = bias_fn(s, ks, c0, w)
        if masked:
            kidx = ks + lax.broadcasted_iota(jnp.int32, (tk, w), 0)
            qidx = q0 + ((c0 + lax.broadcasted_iota(jnp.int32, (tk, w), 1)) & (tq - 1))
            if causal_chunked:
                keep = (kidx >> CHUNK_SHIFT) <= (qidx >> CHUNK_SHIFT)
            else:
                keep = kidx <= qidx
            s = jnp.where(keep, s, NEG_INF)
        return s

    def pv_of(p, j):
        ks = pl.multiple_of(j * tk, tk)
        return lax.dot_general(v_ref[pl.ds(ks, tk), :], p.astype(BF16), (((0,), (0,)), ((), ())),
                               preferred_element_type=F32)

    def update_exact(s, j, c0):
        cols = slice(c0, ncols)
        m_prev = m_scr[:, cols]
        m_new = jnp.maximum(m_prev, jnp.max(s, axis=0, keepdims=True))
        alpha = jnp.exp2(m_prev - m_new)
        p = jnp.exp2(s - m_new)
        l_scr[:, cols] = alpha * l_scr[:, cols] + jnp.sum(p, axis=0, keepdims=True)
        acc_scr[:, cols] = alpha * acc_scr[:, cols] + pv_of(p, j)
        m_scr[:, cols] = m_new

    def update_lagged(s, j, c0):
        cols = slice(c0, ncols)
        m_prev = m_scr[:, cols]
        p = jnp.exp2(s - m_prev)
        t_max = jnp.max(s, axis=0, keepdims=True)
        m_new = jnp.maximum(m_prev, t_max)
        beta = jnp.exp2(m_prev - m_new)
        l_scr[:, cols] = (l_scr[:, cols] + jnp.sum(p, axis=0, keepdims=True)) * beta
        acc_scr[:, cols] = (acc_scr[:, cols] + pv_of(p, j)) * beta
        gap_scr[:, cols] = jnp.maximum(gap_scr[:, cols], t_max - m_prev)
        m_scr[:, cols] = m_new

    def steps(tiles, update):
        s_all = [scores(j, masked, c0) for j, masked, c0 in tiles]
        for s, (j, _, c0) in zip(s_all, tiles):
            update(s, j, c0)

    n_full = q0 // tk
    single_map = ncols == tq
    diag = [(n_full + d, True, d * tk if single_map else 0) for d in range(tq // tk)]

    init()
    for i in range(0, len(diag), 2):
        steps(diag[::-1][i:i + 2], update_exact)

    def body(i, carry):
        steps([(2 * i, False, 0), (2 * i + 1, False, 0)], update_lagged)
        return carry

    lax.fori_loop(0, n_full // 2, body, 0)
    if (tq // tk) % 2:

        @pl.when(n_full % 2 == 1)
        def _():
            steps([(n_full - 1, False, 0)], update_lagged)

    @pl.when(jnp.max(gap_scr[...]) > MAX_GAP)
    def _():
        init()

        def body_exact(j, carry):
            steps([(j, False, 0)], update_exact)
            return carry

        lax.fori_loop(0, n_full, body_exact, 0)
        for tile in diag:
            steps([tile], update_exact)


def _lane_tile(x, ncols):
    return x if ncols == x.shape[1] else jnp.concatenate([x] * (ncols // x.shape[1]), axis=1)


LOG2E = math.log2(math.e)
MAX_GAP = 60.0
N_AUG = 3


def _bf16_pieces(x):
    hi = x.astype(BF16).astype(F32)
    rest = x - hi
    mid = rest.astype(BF16).astype(F32)
    return hi, mid, rest - mid


def _aug_key_cols(col):
    hi, mid, lo = _bf16_pieces(col)
    lane = lax.broadcasted_iota(jnp.int32, col.shape, 1)
    ones = jnp.where(lane < 2 * N_AUG, 1.0, 0.0)
    return jnp.where(lane == 0, hi, jnp.where(lane == 1, mid, jnp.where(lane == 2, lo, ones))).astype(BF16)


def _aug_query_rows(row_vec, n_rows):
    hi, mid, lo = _bf16_pieces(jnp.broadcast_to(row_vec, (n_rows, row_vec.shape[1])))
    row = lax.broadcasted_iota(jnp.int32, hi.shape, 0)
    ones = jnp.where(row < N_AUG, 1.0, 0.0)
    return jnp.where(row == 3, hi, jnp.where(row == 4, mid, jnp.where(row == 5, lo, ones))).astype(BF16)


def _attn_a_kernel(slopes_ref, q_ref, k_ref, v_ref, posq_ref, posk_ref, lam_ref, g_ref, o_ref,
                   kcat_scr, qt_scr, m_scr, l_scr, acc_scr, gap_scr, *, tq, tk, lam_init):
    h = pl.program_id(1)
    q0 = pl.program_id(2) * tq
    q_t = q_ref[...].astype(F32).T
    row = lax.broadcasted_iota(jnp.int32, q_t.shape, 0)
    zero = jnp.zeros_like(q_t)
    qt_scr[:HEAD_DIM, :tq] = jnp.where(row < A_QK_DIM, q_t, zero).astype(BF16)
    qt_scr[:HEAD_DIM, tq:] = jnp.where(row >= A_QK_DIM, q_t, zero).astype(BF16)
    slope = slopes_ref[h] * LOG2E
    pq = _lane_tile(posq_ref[...], 2 * tq)
    qt_scr[HEAD_DIM:, :] = _aug_query_rows(pq * (-slope), HEAD_DIM)

    @pl.when(pl.program_id(2) == 0)
    def _():
        kcat_scr[:, :HEAD_DIM] = k_ref[...]
        kcat_scr[:, HEAD_DIM:] = _aug_key_cols(posk_ref[...] * slope)

    def bias_fn(s, ks, c0, w):
        pk = _lane_tile(posk_ref[pl.ds(ks, tk), :], w)
        return s + (2.0 * slope) * jnp.minimum(pq[:, c0:c0 + w] - pk, 0.0)

    _flash_loop(qt_scr, kcat_scr, v_ref, m_scr, l_scr, acc_scr, gap_scr, q0=q0, tq=tq, tk=tk,
                bias_fn=bias_fn, causal_chunked=True)
    dl = lam_ref[...]
    lam = (jnp.exp(jnp.sum(dl[0:1] * dl[1:2], axis=1, keepdims=True))
           - jnp.exp(jnp.sum(dl[2:3] * dl[3:4], axis=1, keepdims=True)) + lam_init)
    accn = acc_scr[...] * (1.0 / l_scr[...])
    o = (accn[:, :tq] - lam * accn[:, tq:]).T
    o_ref[...] = (_rms(o, g_ref[...]) * (1.0 - lam_init)).astype(o_ref.dtype)


def attn_a_call(proj3, posq_row, posk_rep, slopes, diff_lambda, subln_g, lam_init, tq=512, tk=512):
    b, s_len, _ = proj3.shape
    tq = min(tq, s_len)
    tk = min(tk, tq)
    assert tq % tk == 0 and s_len % tq == 0
    kern = functools.partial(_attn_a_kernel, tq=tq, tk=tk, lam_init=lam_init)
    cb = OFF_A // HEAD_DIM
    return pl.pallas_call(
        kern,
        grid=(b, HEADS, s_len // tq),
        in_specs=[
            pl.BlockSpec(memory_space=pltpu.SMEM),
            pl.BlockSpec((None, tq, HEAD_DIM), lambda bi, h, i: (bi, i, cb + h)),
            pl.BlockSpec((None, s_len, HEAD_DIM), lambda bi, h, i: (bi, 0, cb + HEADS + h)),
            pl.BlockSpec((None, s_len, HEAD_DIM), lambda bi, h, i: (bi, 0, cb + 2 * HEADS + h)),
            pl.BlockSpec((None, 1, tq), lambda bi, h, i: (bi, 0, i)),
            pl.BlockSpec((None, s_len, LANES), lambda bi, h, i: (bi, 0, 0)),
            pl.BlockSpec((4, A_QK_DIM), lambda bi, h, i: (0, 0)),
            pl.BlockSpec((1, HEAD_DIM), lambda bi, h, i: (0, 0)),
        ],
        out_specs=pl.BlockSpec((None, tq, HEAD_DIM), lambda bi, h, i: (bi, i, h)),
        out_shape=jax.ShapeDtypeStruct((b, s_len, HEADS * HEAD_DIM), BF16),
        scratch_shapes=[pltpu.VMEM((s_len, 2 * HEAD_DIM), BF16), pltpu.VMEM((2 * HEAD_DIM, 2 * tq), BF16),
                        pltpu.VMEM((1, 2 * tq), F32),
                        pltpu.VMEM((1, 2 * tq), F32), pltpu.VMEM((HEAD_DIM, 2 * tq), F32),
                        pltpu.VMEM((1, 2 * tq), F32)],
        compiler_params=_cparams(("parallel", "parallel", "arbitrary")),
        name="attn_diff",
    )(slopes, proj3, proj3, proj3, posq_row, posk_rep, diff_lambda, subln_g.reshape(1, HEAD_DIM))


def _attn_b_kernel(q_ref, k_ref, v_ref, cum_ref, o_ref, kcat_scr, qt_scr, m_scr, l_scr, acc_scr, gap_scr, *, tq, tk):
    h = pl.program_id(1)
    qi = pl.program_id(2)

    @pl.when(qi == 0)
    def _():
        lane = lax.broadcasted_iota(jnp.int32, cum_ref.shape, 1)
        col = jnp.sum(jnp.where(lane == h, cum_ref[...], 0.0), axis=1, keepdims=True)
        kcat_scr[:, :HEAD_DIM] = k_ref[...]
        kcat_scr[:, HEAD_DIM:] = _aug_key_cols(jnp.broadcast_to(col * (-LOG2E), cum_ref.shape))

    qt_scr[:HEAD_DIM, :] = q_ref[...].astype(F32).T.astype(BF16)
    cum_t = cum_ref[pl.ds(pl.multiple_of(qi * tq, tq), tq), :].T
    head = lax.broadcasted_iota(jnp.int32, cum_t.shape, 0)
    cq = jnp.sum(jnp.where(head == h, cum_t, 0.0), axis=0, keepdims=True)
    qt_scr[HEAD_DIM:, :] = _aug_query_rows(cq * LOG2E, HEAD_DIM)
    _flash_loop(qt_scr, kcat_scr, v_ref, m_scr, l_scr, acc_scr, gap_scr, q0=qi * tq, tq=tq, tk=tk,
                bias_fn=None, causal_chunked=False)
    o_ref[...] = (acc_scr[...] * (1.0 / l_scr[...])).T.astype(o_ref.dtype)


def attn_b_call(proj3, cum, tq=1024, tk=512):
    b, s_len, _ = proj3.shape
    tq = min(tq, s_len)
    tk = min(tk, tq)
    assert tq % tk == 0 and s_len % tq == 0
    cb = OFF_B // HEAD_DIM
    return pl.pallas_call(
        functools.partial(_attn_b_kernel, tq=tq, tk=tk),
        grid=(b, HEADS, s_len // tq),
        in_specs=[
            pl.BlockSpec((None, tq, HEAD_DIM), lambda bi, h, i: (bi, i, cb + h)),
            pl.BlockSpec((None, s_len, HEAD_DIM), lambda bi, h, i: (bi, 0, cb + HEADS + h)),
            pl.BlockSpec((None, s_len, HEAD_DIM), lambda bi, h, i: (bi, 0, cb + 2 * HEADS + h)),
            pl.BlockSpec((None, s_len, LANES), lambda bi, h, i: (bi, 0, 0)),
        ],
        out_specs=pl.BlockSpec((None, tq, HEAD_DIM), lambda bi, h, i: (bi, i, h)),
        out_shape=jax.ShapeDtypeStruct((b, s_len, HEADS * HEAD_DIM), BF16),
        scratch_shapes=[pltpu.VMEM((s_len, 2 * HEAD_DIM), BF16), pltpu.VMEM((2 * HEAD_DIM, tq), BF16),
                        pltpu.VMEM((1, tq), F32), pltpu.VMEM((1, tq), F32), pltpu.VMEM((HEAD_DIM, tq), F32),
                        pltpu.VMEM((1, tq), F32)],
        compiler_params=_cparams(("parallel", "parallel", "arbitrary")),
        name="attn_fox",
    )(proj3, proj3, proj3, cum)


def _attn_c_kernel(q_ref, k_ref, v_ref, o_ref, qt_scr, m_scr, l_scr, acc_scr, gap_scr, *, tq, tk):
    qt_scr[...] = q_ref[...].astype(F32).T.astype(BF16)
    _flash_loop(qt_scr, k_ref, v_ref, m_scr, l_scr, acc_scr, gap_scr,q0=pl.program_id(2) * tq, tq=tq, tk=tk,
                bias_fn=None, causal_chunked=True)
    o_ref[...] = (acc_scr[...] * (1.0 / l_scr[...])).T.astype(o_ref.dtype)


def attn_c_call(q3, k3, v3, tq=1024, tk=512):
    b, s_len, _ = q3.shape
    tq = min(tq, s_len)
    tk = min(tk, tq)
    assert tq % tk == 0 and s_len % tq == 0
    return pl.pallas_call(
        functools.partial(_attn_c_kernel, tq=tq, tk=tk),
        grid=(b, HEADS, s_len // tq),
        in_specs=[
            pl.BlockSpec((None, tq, C_QK_PAD), lambda bi, h, i: (bi, i, h)),
            pl.BlockSpec((None, s_len, C_QK_PAD), lambda bi, h, i: (bi, 0, h)),
            pl.BlockSpec((None, s_len, HEAD_DIM), lambda bi, h, i: (bi, 0, h)),
        ],
        out_specs=pl.BlockSpec((None, tq, HEAD_DIM), lambda bi, h, i: (bi, i, h)),
        out_shape=jax.ShapeDtypeStruct((b, s_len, HEADS * HEAD_DIM), BF16),
        scratch_shapes=[pltpu.VMEM((C_QK_PAD, tq), BF16), pltpu.VMEM((1, tq), F32), pltpu.VMEM((1, tq), F32),
                        pltpu.VMEM((HEAD_DIM, tq), F32), pltpu.VMEM((1, tq), F32)],
        compiler_params=_cparams(("parallel", "parallel", "arbitrary")),
        name="attn_mla",
    )(q3, k3, v3)


def _merge_kernel(x_ref, oa_ref, ob_ref, oc_ref, ga_ref, gb_ref, gc_ref, wb_ref, wo_ref, g_ref,
                  x_out_ref, h_out_ref):
    merged = None
    for gi, (o_ref, gate_ref) in enumerate(((oa_ref, ga_ref), (ob_ref, gb_ref), (oc_ref, gc_ref))):
        br = jnp.dot(o_ref[...], wb_ref[gi], preferred_element_type=F32)
        term = jax.nn.sigmoid(gate_ref[...].astype(F32)) * br
        merged = term if merged is None else merged + term
    x_new = x_ref[...] + jnp.dot(merged.astype(BF16), wo_ref[...], preferred_element_type=F32)
    x_out_ref[...] = x_new
    h_out_ref[...] = _rms(x_new, g_ref[...]).astype(h_out_ref.dtype)


def merge_call(x, o_a, o_b, o_c, proj, w_branch, w_out, norm_g, tm=256):
    m, d = x.shape
    tm = min(tm, m)
    row = lambda width: pl.BlockSpec((tm, width), lambda i: (i, 0))
    gate = lambda gi: pl.BlockSpec((tm, d), lambda i: (i, OFF_GATES // d + gi))
    return pl.pallas_call(
        _merge_kernel,
        grid=(m // tm,),
        in_specs=[
            row(d), row(BRANCH_WIDTH), row(BRANCH_WIDTH), row(BRANCH_WIDTH), gate(0), gate(1), gate(2),
            pl.BlockSpec(w_branch.shape, lambda i: (0, 0, 0), pipeline_mode=pl.Buffered(1)),
            pl.BlockSpec(w_out.shape, lambda i: (0, 0), pipeline_mode=pl.Buffered(1)),
            pl.BlockSpec((1, d), lambda i: (0, 0)),
        ],
        out_specs=[row(d), row(d)],
        out_shape=[jax.ShapeDtypeStruct((m, d), F32), jax.ShapeDtypeStruct((m, d), BF16)],
        compiler_params=_cparams(("parallel",)),
        name="merge_out",
    )(x, o_a, o_b, o_c, proj, proj, proj, w_branch, w_out, norm_g.reshape(1, d))


def _ffn_kernel(h_ref, x_ref, w1_ref, w3_ref, w2_ref, g_ref, *outs, emit_x):
    acc_ref = outs[-1]
    f = pl.program_id(1)

    @pl.when(f == 0)
    def _():
        acc_ref[...] = x_ref[...]

    h = h_ref[...]
    a = jnp.dot(h, w1_ref[...], preferred_element_type=F32)
    b = jnp.dot(h, w3_ref[...], preferred_element_type=F32)
    act = (a * jax.nn.sigmoid(a) * b).astype(BF16)
    acc_ref[...] += jnp.dot(act, w2_ref[...], preferred_element_type=F32)

    @pl.when(f == pl.num_programs(1) - 1)
    def _():
        x_new = acc_ref[...]
        if emit_x:
            outs[0][...] = x_new
        outs[-2][...] = _rms(x_new, g_ref[...]).astype(outs[-2].dtype)


def ffn_call(h, x, w1, w3, w2, norm_g, *, emit_x, norm_dtype, tm=512, tf=512):
    m, d = x.shape
    tm = min(tm, m)
    row = pl.BlockSpec((tm, d), lambda i, f: (i, 0))
    w_in = pl.BlockSpec((d, tf), lambda i, f: (0, f))
    out_specs = [row]
    out_shape = [jax.ShapeDtypeStruct((m, d), norm_dtype)]
    if emit_x:
        out_specs = [row, row]
        out_shape = [jax.ShapeDtypeStruct((m, d), F32)] + out_shape
    return pl.pallas_call(
        functools.partial(_ffn_kernel, emit_x=emit_x),
        grid=(m // tm, w1.shape[1] // tf),
        in_specs=[row, row, w_in, w_in, pl.BlockSpec((tf, d), lambda i, f: (f, 0)),
                  pl.BlockSpec((1, d), lambda i, f: (0, 0))],
        out_specs=out_specs,
        out_shape=out_shape,
        scratch_shapes=[pltpu.VMEM((tm, d), F32)],
        compiler_params=_cparams(("parallel", "arbitrary")),
        name="dense_ffn",
    )(h, x, w1, w3, w2, norm_g.reshape(1, d))


def _router_kernel(x_ref, g_ref, w_ref, i_ref, c_ref):
    h = _rms(x_ref[...], g_ref[...])
    logits = jnp.dot(h, w_ref[...], preferred_element_type=F32, precision=lax.Precision.HIGHEST)
    lane = lax.broadcasted_iota(jnp.int32, logits.shape, 1)
    logits = jnp.where(lane < N_EXPERTS, logits, -jnp.inf)
    m1 = jnp.max(logits, axis=1, keepdims=True)
    i1 = jnp.min(jnp.where(logits == m1, lane, LANES), axis=1, keepdims=True)
    rest = jnp.where(lane == i1, -jnp.inf, logits)
    m2 = jnp.max(rest, axis=1, keepdims=True)
    i2 = jnp.min(jnp.where(rest == m2, lane, LANES), axis=1, keepdims=True)
    e2 = jnp.exp(m2 - m1)
    inv = 1.0 / (1.0 + e2)
    i_ref[...] = jnp.where(lane == 0, i1, jnp.where(lane == 1, i2, 0))
    c_ref[...] = jnp.where(lane == 0, inv, jnp.where(lane == 1, e2 * inv, 0.0))


def router_call(x, norm_g, w_pad, tm=512):
    m, d = x.shape
    tm = min(tm, m)
    out = pl.BlockSpec((tm, LANES), lambda i: (i, 0))
    return pl.pallas_call(
        _router_kernel,
        grid=(m // tm,),
        in_specs=[pl.BlockSpec((tm, d), lambda i: (i, 0)), pl.BlockSpec((1, d), lambda i: (0, 0)),
                  pl.BlockSpec((d, LANES), lambda i: (0, 0))],
        out_specs=[out, out],
        out_shape=[jax.ShapeDtypeStruct((m, LANES), jnp.int32), jax.ShapeDtypeStruct((m, LANES), F32)],
        compiler_params=_cparams(("parallel",)),
        name="router",
    )(x, norm_g.reshape(1, d), w_pad)


def _row_copy(src_hbm, idx_ref, dst_ref, sem, r):
    return pltpu.make_async_copy(src_hbm.at[pl.ds(idx_ref[0, r], 1)], dst_ref.at[pl.ds(r, 1)], sem)


def _gather_rows(src_hbm, idx_ref, dst_ref, sem):
    n = dst_ref.shape[0]

    def start(r, c):
        _row_copy(src_hbm, idx_ref, dst_ref, sem, r).start()
        return c

    def wait(r, c):
        _row_copy(src_hbm, idx_ref, dst_ref, sem, r).wait()
        return c

    lax.fori_loop(0, n, start, 0, unroll=8)
    lax.fori_loop(0, n, wait, 0, unroll=8)


def _gather_kernel(idx_ref, src_hbm, o_ref, sem):
    _gather_rows(src_hbm, idx_ref, o_ref, sem)


def gather_rows_call(src, idx, tg=256):
    n_rows = idx.shape[0]
    width = src.shape[1]
    return pl.pallas_call(
        _gather_kernel,
        grid=(n_rows // tg,),
        in_specs=[pl.BlockSpec((None, 1, tg), lambda i: (i, 0, 0), memory_space=pltpu.SMEM),
                  pl.BlockSpec(memory_space=pl.ANY)],
        out_specs=pl.BlockSpec((tg, width), lambda i: (i, 0)),
        out_shape=jax.ShapeDtypeStruct((n_rows, width), src.dtype),
        scratch_shapes=[pltpu.SemaphoreType.DMA],
        compiler_params=_cparams(("arbitrary",)),
        name="moe_gather",
    )(idx.reshape(n_rows // tg, 1, tg), src)


def _moe_group_kernel(te_ref, nv_ref, xs_ref, g_ref, w1_ref, w3_ref, w2_ref, o_ref, h_scr, acc_scr):
    del te_ref
    f = pl.program_id(1)
    valid = pl.program_id(0) < nv_ref[0]
    last = f == pl.num_programs(1) - 1

    @pl.when(valid & (f == 0))
    def _():
        h_scr[...] = _rms(xs_ref[...], g_ref[...]).astype(h_scr.dtype)
        acc_scr[...] = jnp.zeros(acc_scr.shape, F32)

    @pl.when(valid)
    def _():
        h = h_scr[...]
        a = jnp.dot(h, w1_ref[...], preferred_element_type=F32)
        b = jnp.dot(h, w3_ref[...], preferred_element_type=F32)
        act = (a * jax.nn.sigmoid(a) * b).astype(BF16)
        acc_scr[...] += jnp.dot(act, w2_ref[...], preferred_element_type=F32)

    @pl.when(valid & last)
    def _():
        o_ref[...] = acc_scr[...]

    @pl.when(jnp.logical_not(valid) & last)
    def _():
        o_ref[...] = jnp.zeros(o_ref.shape, o_ref.dtype)


def moe_group_call(xs, norm_g, w1, w3, w2, tile_expert, n_valid, tm, tf=1024):
    mp, d = xs.shape
    n_f = w1.shape[-1] // tf

    def f_sel(i, f, nv):
        return jnp.where(i < nv[0], f, n_f - 1)

    w_in = pl.BlockSpec((None, d, tf), lambda i, f, te, nv: (te[i], 0, f_sel(i, f, nv)))
    w_out = pl.BlockSpec((None, tf, d), lambda i, f, te, nv: (te[i], f_sel(i, f, nv), 0))
    row = pl.BlockSpec((tm, d), lambda i, f, te, nv: (i, 0))
    return pl.pallas_call(
        _moe_group_kernel,
        grid_spec=pltpu.PrefetchScalarGridSpec(
            num_scalar_prefetch=2,
            grid=(mp // tm, n_f),
            in_specs=[row, pl.BlockSpec((1, d), lambda i, f, te, nv: (0, 0)), w_in, w_in, w_out],
            out_specs=row,
            scratch_shapes=[pltpu.VMEM((tm, d), BF16), pltpu.VMEM((tm, d), F32)],
        ),
        out_shape=jax.ShapeDtypeStruct((mp, d), F32),
        compiler_params=_cparams(("arbitrary", "arbitrary")),
        name="moe_experts",
    )(tile_expert, n_valid, xs, norm_g.reshape(1, d), w1, w3, w2)


def _moe_combine_kernel(p0_ref, p1_ref, x_ref, c_ref, g_ref, ys_hbm, *rest, emit_x):
    *outs, y0_scr, y1_scr, sem0, sem1 = rest
    n = x_ref.shape[0]

    def start(r, c):
        _row_copy(ys_hbm, p0_ref, y0_scr, sem0, r).start()
        _row_copy(ys_hbm, p1_ref, y1_scr, sem1, r).start()
        return c

    def wait(r, c):
        _row_copy(ys_hbm, p0_ref, y0_scr, sem0, r).wait()
        _row_copy(ys_hbm, p1_ref, y1_scr, sem1, r).wait()
        return c

    lax.fori_loop(0, n, start, 0, unroll=8)
    lax.fori_loop(0, n, wait, 0, unroll=8)
    lane = lax.broadcasted_iota(jnp.int32, c_ref.shape, 1)
    c = c_ref[...]
    c0 = jnp.sum(jnp.where(lane == 0, c, 0.0), axis=1, keepdims=True)
    c1 = jnp.sum(jnp.where(lane == 1, c, 0.0), axis=1, keepdims=True)
    x_new = x_ref[...] + c0 * y0_scr[...] + c1 * y1_scr[...]
    if emit_x:
        outs[0][...] = x_new
    outs[-1][...] = _rms(x_new, g_ref[...]).astype(outs[-1].dtype)


def moe_combine_call(x, weights, ys, pos0, pos1, norm_g, *, emit_x, norm_dtype, tg=256):
    m, d = x.shape
    tg = min(tg, m)
    idx = pl.BlockSpec((None, 1, tg), lambda i: (i, 0, 0), memory_space=pltpu.SMEM)
    row = pl.BlockSpec((tg, d), lambda i: (i, 0))
    out_specs = [row]
    out_shape = [jax.ShapeDtypeStruct((m, d), norm_dtype)]
    if emit_x:
        out_specs = [row, row]
        out_shape = [jax.ShapeDtypeStruct((m, d), F32)] + out_shape
    return pl.pallas_call(
        functools.partial(_moe_combine_kernel, emit_x=emit_x),
        grid=(m // tg,),
        in_specs=[idx, idx, row, pl.BlockSpec((tg, LANES), lambda i: (i, 0)),
                  pl.BlockSpec((1, d), lambda i: (0, 0)), pl.BlockSpec(memory_space=pl.ANY)],
        out_specs=out_specs,
        out_shape=out_shape,
        scratch_shapes=[pltpu.VMEM((tg, d), F32), pltpu.VMEM((tg, d), F32),
                        pltpu.SemaphoreType.DMA, pltpu.SemaphoreType.DMA],
        compiler_params=_cparams(("arbitrary",)),
        name="moe_combine",
    )(pos0.reshape(m // tg, 1, tg), pos1.reshape(m // tg, 1, tg), x, weights, norm_g.reshape(1, d), ys)


def _route(choice, tm):
    t = choice.shape[0]
    n_tiles = (TOP_K * t) // tm + N_EXPERTS
    e = choice.reshape(-1)
    onehot = (e[:, None] == jnp.arange(N_EXPERTS, dtype=jnp.int32)[None, :]).astype(jnp.int32)
    rank = jnp.cumsum(onehot, axis=0) - onehot
    counts = jnp.sum(onehot, axis=0)
    tiles_per = (counts + tm - 1) // tm
    tile_end = jnp.cumsum(tiles_per)
    start_row = (tile_end - tiles_per) * tm
    pos = jnp.sum(onehot * (start_row[None, :] + rank), axis=1)
    src = jnp.zeros((n_tiles * tm,), jnp.int32).at[pos].set(jnp.arange(TOP_K * t, dtype=jnp.int32) // TOP_K)
    n_valid = tile_end[-1:]
    tile_id = jnp.minimum(jnp.arange(n_tiles, dtype=jnp.int32), n_valid[0] - 1)
    tile_expert = jnp.sum((tile_id[:, None] >= tile_end[None, :]).astype(jnp.int32), axis=1)
    return pos.reshape(t, TOP_K), src, tile_expert.astype(jnp.int32), n_valid.astype(jnp.int32)


def _rot_pair(w):
    half = w.shape[-1] // 2
    return jnp.concatenate([-w[..., half:], w[..., :half]], axis=-1)


def _pack_in_proj(w):
    d = w.shape[0]
    bw = BRANCH_WIDTH
    aq, ak, av, bq, bk, bv = (w[:, i * bw:(i + 1) * bw] for i in range(6))
    o = 6 * bw
    bf = w[:, o:o + HEADS]
    o += HEADS
    cq = w[:, o:o + C_RANK]
    ckv = w[:, o + C_RANK:o + 2 * C_RANK]
    o += 2 * C_RANK
    ckr = w[:, o:o + C_ROPE_DIM]
    gates = w[:, o + C_ROPE_DIM:]
    z = lambda n: jnp.zeros((d, n), w.dtype)
    packed = jnp.concatenate(
        [gates, aq * (A_QK_DIM ** -0.5 * LOG2E), ak, av, bq * (HEAD_DIM ** -0.5 * LOG2E), bk, bv, cq, ckv,
         ckr, z(LANES - C_ROPE_DIM), _rot_pair(ckr), z(LANES - C_ROPE_DIM),
         bf, z(LANES - HEADS), z(N_PROJ - OFF_BF - LANES)], axis=1)
    return packed.astype(BF16)


def _pack_mla(w_uq, w_uk, w_uv):
    r = w_uq.shape[0]
    scale = (C_NOPE_DIM + C_ROPE_DIM) ** -0.5 * LOG2E
    nope, rope = w_uq[..., :C_NOPE_DIM] * scale, w_uq[..., C_NOPE_DIM:] * scale
    zpad = jnp.zeros((r, HEADS, C_QK_PAD - C_NOPE_DIM - C_ROPE_DIM), w_uq.dtype)
    wq = jnp.concatenate([nope, rope, zpad], axis=-1).reshape(r, HEADS * C_QK_PAD)
    wqp = jnp.concatenate([_rot_pair(rope), zpad], axis=-1).reshape(r, HEADS * LANES)
    return (wq.astype(BF16), wqp.astype(BF16), w_uk.reshape(r, -1).astype(BF16),
            w_uv.reshape(r, -1).astype(BF16))


def _rope_freqs():
    half = C_ROPE_DIM // 2
    inv = ROPE_THETA ** (-jnp.arange(half, dtype=F32) / half)
    return jnp.concatenate([inv, inv, jnp.zeros((LANES - C_ROPE_DIM,), F32)]).reshape(1, LANES)


def kernel(x, positions, attn_norm_g, w_in, diff_lambda, diff_subln_g, fox_forget_bias, mla_q_norm_g, mla_kv_norm_g, mla_w_uq, mla_w_uk, mla_w_uv, w_branch, w_out, ffn_norm_g, dense_w1, dense_w3, dense_w2, router_w, expert_w1, expert_w3, expert_w2, final_norm_g):
    b, s_len, d = x.shape
    m = b * s_len
    depth = w_in.shape[0]
    posf = positions.astype(F32)
    posq_row = posf.reshape(b, 1, s_len)
    posk_rep = jnp.broadcast_to(posf[:, :, None], (b, s_len, LANES))
    pos_col = posf.reshape(m, 1)
    slopes = jnp.exp2(-8.0 * (jnp.arange(HEADS, dtype=F32) + 1.0) / HEADS)
    freqs = _rope_freqs()

    xr = x.reshape(m, d)
    hn = rmsnorm_call(xr, attn_norm_g[0], BF16)
    out = None
    for l in range(depth):
        lam_init = 0.8 - 0.6 * math.exp(-0.3 * l)
        proj = matmul_call(hn, _pack_in_proj(w_in[l]), BF16)
        proj3 = proj.reshape(b, s_len, N_PROJ)

        o_a = attn_a_call(proj3, posq_row, posk_rep, slopes, diff_lambda[l], diff_subln_g[l], lam_init)
        bias_row = jnp.zeros((1, LANES), F32).at[0, :HEADS].set(fox_forget_bias[l])
        o_b = attn_b_call(proj3, fox_cumsum_call(proj3, bias_row))
        wq, wqp, wk, wv = _pack_mla(mla_w_uq[l], mla_w_uk[l], mla_w_uv[l])
        q_c, k_c, v_c = mla_up_call(proj, pos_col, mla_q_norm_g[l].reshape(1, -1),
                                    mla_kv_norm_g[l].reshape(1, -1), freqs, wq, wqp, wk, wv)
        o_c = attn_c_call(q_c.reshape(b, s_len, -1), k_c.reshape(b, s_len, -1), v_c.reshape(b, s_len, -1))

        x1, h_ffn = merge_call(xr, o_a.reshape(m, -1), o_b.reshape(m, -1), o_c.reshape(m, -1), proj,
                               w_branch[l].astype(BF16), w_out[l].astype(BF16), ffn_norm_g[l])
        last = l == depth - 1
        next_g = final_norm_g if last else attn_norm_g[l + 1]
        j = l // 2
        if l % 2 == 0:
            res = ffn_call(h_ffn, x1, dense_w1[j].astype(BF16), dense_w3[j].astype(BF16),
                           dense_w2[j].astype(BF16), next_g, emit_x=not last,
                           norm_dtype=F32 if last else BF16)
        else:
            w_pad = jnp.zeros((d, LANES), F32).at[:, :N_EXPERTS].set(router_w[j])
            choice, weights = router_call(x1, ffn_norm_g[l], w_pad)
            tm_moe = min(512, m)
            pos, src, tile_expert, n_valid = _route(choice[:, :TOP_K], tm_moe)
            xs = gather_rows_call(x1, src)
            ys = moe_group_call(xs, ffn_norm_g[l], expert_w1[j].astype(BF16), expert_w3[j].astype(BF16),
                                expert_w2[j].astype(BF16), tile_expert, n_valid, tm_moe)
            res = moe_combine_call(x1, weights, ys, pos[:, 0], pos[:, 1], next_g, emit_x=not last,
                                   norm_dtype=F32 if last else BF16)
        if last:
            out = res[0]
        else:
            xr, hn = res
    return out.reshape(b, s_len, d)
```
